```python
import functools
import jax, jax.numpy as jnp
from jax import lax
import numpy as np

D_MODEL = 1024
BATCH = 4
SEQ = 8192
DEPTH = 2
DEC_BATCH = 8
DEC_SEQ = 32
PAST_LEN = 4096

CHUNK = 64
N_META = 16
EPS = 1e-6
D_RNN = D_MODEL
RNN_BLOCKS = 16
RNN_BLOCK = D_RNN // RNN_BLOCKS
CONV_W = 4
LRU_C = 8.0
HEAD_DIM = 64
N_HEADS = D_MODEL // HEAD_DIM
N_KV = 4
GQA_GROUP = N_HEADS // N_KV
D_ATTN = N_HEADS * HEAD_DIM
D_KV = N_KV * HEAD_DIM
WINDOW = 128
W_CHUNKS = WINDOW // CHUNK
D_POOL = D_MODEL
POOL_SIZES = (2, 4, 8, 16)
N_POOL_GROUPS = len(POOL_SIZES)
POOL_GROUP = D_POOL // N_POOL_GROUPS
POOL_MAX = max(POOL_SIZES)
N_BRANCH = 3
D_BRANCH = D_MODEL
PEER_HEADS = 8
PEER_KEYS = 128
N_EXPERTS = PEER_KEYS * PEER_KEYS
PEER_QDIM = 256
PEER_HALF = PEER_QDIM // 2
PEER_TOPK = 16
PEER_BLOCK = 256
IN_SIZES = (D_RNN, D_RNN, D_ATTN, D_KV, D_KV, D_POOL, N_BRANCH * D_MODEL)
D_IN = sum(IN_SIZES)
IN_SPLITS = (D_RNN, 2 * D_RNN, 2 * D_RNN + D_ATTN, 2 * D_RNN + D_ATTN + D_KV, 2 * D_RNN + D_ATTN + 2 * D_KV, 2 * D_RNN + D_ATTN + 2 * D_KV + D_POOL)
NEG_INF = -1e30

kernel_name = 'hybrid_streaming_encoder_step'


def rmsnorm(x, g):
    xf = x.astype(jnp.float32)
    xf = xf * lax.rsqrt(jnp.mean(xf * xf, axis=-1, keepdims=True) + EPS)
    return xf.astype(x.dtype) * g.astype(x.dtype)


def causal_conv(hist, x, w, b):
    full = jnp.concatenate([hist, x], axis=1)
    T = x.shape[1]
    y = b
    for j in range(CONV_W):
        y = y + full[:, j:j + T] * w[j]
    return y, full[:, -(CONV_W - 1):]


def rg_lru(x, h0, wa, ba, wx, bx, lam):
    B, T, _ = x.shape
    xb = x.reshape(B, T, RNN_BLOCKS, RNN_BLOCK)
    r = jax.nn.sigmoid((jnp.einsum('btni,nij->btnj', xb, wa).reshape(B, T, D_RNN) + ba).astype(jnp.float32))
    i = jax.nn.sigmoid((jnp.einsum('btni,nij->btnj', xb, wx).reshape(B, T, D_RNN) + bx).astype(jnp.float32))
    log_a = -LRU_C * jax.nn.softplus(-lam.astype(jnp.float32)) * r
    a = jnp.exp(log_a)
    u = jnp.sqrt(-jnp.expm1(2.0 * log_a)) * (i * x.astype(jnp.float32))
    u = u.at[:, 0].add(a[:, 0] * h0.astype(jnp.float32))

    def combine(left, right):
        return (left[0] * right[0], right[0] * left[1] + right[1])

    _, h = lax.associative_scan(combine, (a, u), axis=1)
    return h.astype(x.dtype), h[:, -1].astype(x.dtype)


def sink_softmax(logits, sinks):
    sk = sinks.astype(jnp.float32).reshape(N_KV, GQA_GROUP, 1, 1)
    m = jnp.maximum(jnp.max(logits, axis=-1, keepdims=True), sk)
    e = jnp.exp(logits - m)
    return e / (jnp.sum(e, axis=-1, keepdims=True) + jnp.exp(sk - m))


def attend_prompt(q, k, v, sinks):
    B, S = q.shape[0], q.shape[1]
    T = S - N_META
    NC = T // CHUNK
    scale = HEAD_DIM ** -0.5
    qm = q[:, :N_META].reshape(B, N_META, N_KV, GQA_GROUP, HEAD_DIM)
    lm = jnp.einsum('bqkgd,bskd->bkgqs', qm, k[:, :N_META], preferred_element_type=jnp.float32) * scale
    pm = sink_softmax(lm, sinks).astype(v.dtype)
    om = jnp.einsum('bkgqs,bskd->bqkgd', pm, v[:, :N_META]).reshape(B, N_META, N_HEADS, HEAD_DIM)
    qc = q[:, N_META:].reshape(B, NC, CHUNK, N_KV, GQA_GROUP, HEAD_DIM)

    def band(t):
        tc = jnp.pad(t[:, N_META:].reshape(B, NC, CHUNK, N_KV, HEAD_DIM), ((0, 0), (W_CHUNKS, 0), (0, 0), (0, 0), (0, 0)))
        rows = [jnp.broadcast_to(t[:, None, :N_META], (B, NC, N_META, N_KV, HEAD_DIM))]
        rows += [tc[:, j:j + NC] for j in range(W_CHUNKS + 1)]
        return jnp.concatenate(rows, axis=2)

    kb, vb = band(k), band(v)
    key_chunk = np.arange(NC)[:, None] - W_CHUNKS + np.arange(W_CHUNKS + 1)[None, :]
    valid = np.concatenate([np.ones((NC, N_META), dtype=bool), np.repeat(key_chunk >= 0, CHUNK, axis=1)], axis=1)
    lr = jnp.einsum('bcqkgd,bcskd->bckgqs', qc, kb, preferred_element_type=jnp.float32) * scale
    lr = jnp.where(jnp.asarray(valid)[None, :, None, None, None, :], lr, NEG_INF)
    pr = sink_softmax(lr, sinks).astype(v.dtype)
    orr = jnp.einsum('bckgqs,bcskd->bcqkgd', pr, vb).reshape(B, T, N_HEADS, HEAD_DIM)
    return jnp.concatenate([om, orr], axis=1)


def attend_sample(meta_k, meta_v, win_k, win_v, q, k, v, sinks):
    B, T = q.shape[0], q.shape[1]
    kk = jnp.concatenate([meta_k, win_k, k], axis=1)
    vv = jnp.concatenate([meta_v, win_v, v], axis=1)
    qg = q.reshape(B, T, N_KV, GQA_GROUP, HEAD_DIM)
    lg = jnp.einsum('bqkgd,bskd->bkgqs', qg, kk, preferred_element_type=jnp.float32) * (HEAD_DIM ** -0.5)
    p = sink_softmax(lg, sinks).astype(v.dtype)
    return jnp.einsum('bkgqs,bskd->bqkgd', p, vv).reshape(B, T, N_HEADS, HEAD_DIM)


def pool_mix(hist, x, w, scale):
    P, T = hist.shape[1], x.shape[1]
    full = jnp.concatenate([hist, x], axis=1)
    cs = jnp.pad(jnp.cumsum(full.astype(jnp.float32), axis=1), ((0, 0), (1, 0), (0, 0)))
    end = np.arange(P + 1, P + T + 1)
    outs = []
    for g, size in enumerate(POOL_SIZES):
        start = np.maximum(end - size, 0)
        cnt = jnp.asarray((end - start).astype(np.float32))[None, :, None]
        lo, hi = g * POOL_GROUP, (g + 1) * POOL_GROUP
        mean = (cs[:, end, lo:hi] - cs[:, start, lo:hi]) / cnt
        outs.append(mean - x[:, :, lo:hi].astype(jnp.float32))
    pooled = jnp.stack(outs, axis=2).astype(x.dtype)
    y = jnp.einsum('btgi,gij->btgj', pooled, w).reshape(x.shape[0], T, D_POOL) * scale
    return y, full[:, -(POOL_MAX - 1):]


def token_mixers(h, conv_hist, lru_h0, pool_hist, attend, w):
    B, T, _ = h.shape
    xr, gr, q, k, v, xpool, gl = jnp.split(h @ w['w_in'], IN_SPLITS, axis=-1)
    xc, conv_new = causal_conv(conv_hist, xr, w['conv_w'], w['conv_b'])
    y_lru, h_last = rg_lru(xc, lru_h0, w['lru_wa'], w['lru_ba'], w['lru_wx'], w['lru_bx'], w['lru_lambda'])
    ya = y_lru * jax.nn.gelu(gr, approximate=True)
    q = rmsnorm(q.reshape(B, T, N_HEADS, HEAD_DIM), w['q_norm'])
    k = rmsnorm(k.reshape(B, T, N_KV, HEAD_DIM), w['k_norm'])
    v = v.reshape(B, T, N_KV, HEAD_DIM)
    yb = attend(q, k, v, w['attn_sinks']).reshape(B, T, D_ATTN)
    yc, pool_new = pool_mix(pool_hist, xpool, w['pool_w'], w['pool_scale'])
    proj = jnp.einsum('btnc,ncd->btnd', jnp.stack([ya, yb, yc], axis=2), w['w_branch'])
    gates = jax.nn.sigmoid(gl.reshape(B, T, N_BRANCH, D_MODEL) + w['gate_b'])
    merged = jnp.sum(gates * proj, axis=2)
    return merged @ w['w_out'], conv_new, h_last, pool_new, k, v


def peer_ffn(h, wq, sub_keys, u, v):
    B, T, D = h.shape
    N = B * T
    nb = -(-N // PEER_BLOCK)
    blocks = jnp.pad(h.reshape(N, D), ((0, nb * PEER_BLOCK - N), (0, 0))).reshape(nb, PEER_BLOCK, D)

    def one_block(xb):
        q = (xb @ wq).reshape(PEER_BLOCK, PEER_HEADS, 2, PEER_HALF)
        s = jnp.einsum('thpd,hpnd->thpn', q, sub_keys, preferred_element_type=jnp.float32)
        s_top, i_top = lax.top_k(s, PEER_TOPK)
        cand = s_top[:, :, 0, :, None] + s_top[:, :, 1, None, :]
        cidx = i_top[:, :, 0, :, None] * PEER_KEYS + i_top[:, :, 1, None, :]
        best, pos = lax.top_k(cand.reshape(PEER_BLOCK, PEER_HEADS, PEER_TOPK * PEER_TOPK), PEER_TOPK)
        eidx = jnp.take_along_axis(cidx.reshape(PEER_BLOCK, PEER_HEADS, PEER_TOPK * PEER_TOPK), pos, axis=-1)
        gate = jax.nn.softmax(best, axis=-1).reshape(PEER_BLOCK, PEER_HEADS * PEER_TOPK)
        eflat = eidx.reshape(PEER_BLOCK, PEER_HEADS * PEER_TOPK)
        act = jax.nn.gelu(jnp.einsum('td,ted->te', xb, u[eflat]), approximate=False)
        coef = gate.astype(xb.dtype) * act
        return jnp.einsum('te,ted->td', coef, v[eflat])

    out = lax.map(one_block, blocks).reshape(nb * PEER_BLOCK, D)[:N]
    return out.reshape(B, T, D)


def setup_inputs(seed: int = 0) -> dict:
    key = jax.random.key(seed)
    ks = jax.random.split(key, 32)
    f32 = jnp.float32

    def nrm(i, shape, scale):
        return jax.random.normal(ks[i], shape, f32) * scale

    lam_u = jax.random.uniform(ks[13], (DEPTH, D_RNN), f32, 0.9, 0.999)
    lam_a = lam_u ** (1.0 / LRU_C)
    return {
        'x_prompt': nrm(0, (BATCH, SEQ, D_MODEL), 1.0),
        'x_sample': nrm(1, (DEC_BATCH, DEC_SEQ, D_MODEL), 1.0),
        'state_rglru_h': nrm(2, (DEPTH, DEC_BATCH, D_RNN), 0.5),
        'state_rglru_conv': nrm(3, (DEPTH, DEC_BATCH, CONV_W - 1, D_RNN), 1.0),
        'cache_meta_k': nrm(4, (DEPTH, DEC_BATCH, N_META, N_KV, HEAD_DIM), 1.0),
        'cache_meta_v': nrm(5, (DEPTH, DEC_BATCH, N_META, N_KV, HEAD_DIM), 1.0),
        'cache_win_k': nrm(6, (DEPTH, DEC_BATCH, WINDOW, N_KV, HEAD_DIM), 1.0),
        'cache_win_v': nrm(7, (DEPTH, DEC_BATCH, WINDOW, N_KV, HEAD_DIM), 1.0),
        'state_pool': nrm(8, (DEPTH, DEC_BATCH, POOL_MAX - 1, D_POOL), 1.0),
        'meta_tokens': nrm(9, (N_META, D_MODEL), 1.0),
        'norm_mix': 1.0 + nrm(10, (DEPTH, D_MODEL), 0.01),
        'norm_ffn': 1.0 + nrm(11, (DEPTH, D_MODEL), 0.01),
        'w_in': nrm(12, (DEPTH, D_MODEL, D_IN), D_MODEL ** -0.5),
        'conv_w': nrm(14, (DEPTH, CONV_W, D_RNN), CONV_W ** -0.5),
        'conv_b': nrm(15, (DEPTH, D_RNN), 0.01),
        'lru_wa': nrm(16, (DEPTH, RNN_BLOCKS, RNN_BLOCK, RNN_BLOCK), RNN_BLOCK ** -0.5),
        'lru_ba': nrm(17, (DEPTH, D_RNN), 0.01),
        'lru_wx': nrm(18, (DEPTH, RNN_BLOCKS, RNN_BLOCK, RNN_BLOCK), RNN_BLOCK ** -0.5),
        'lru_bx': nrm(19, (DEPTH, D_RNN), 0.01),
        'lru_lambda': jnp.log(lam_a) - jnp.log1p(-lam_a),
        'q_norm': 1.0 + nrm(20, (DEPTH, HEAD_DIM), 0.01),
        'k_norm': 1.0 + nrm(21, (DEPTH, HEAD_DIM), 0.01),
        'attn_sinks': nrm(22, (DEPTH, N_HEADS), 0.5),
        'pool_w': nrm(23, (DEPTH, N_POOL_GROUPS, POOL_GROUP, POOL_GROUP), POOL_GROUP ** -0.5),
        'pool_scale': 1.0 + nrm(24, (DEPTH, D_POOL), 0.1),
        'gate_b': nrm(25, (DEPTH, N_BRANCH, D_MODEL), 0.01),
        'w_branch': nrm(26, (DEPTH, N_BRANCH, D_BRANCH, D_MODEL), D_BRANCH ** -0.5),
        'w_out': nrm(27, (DEPTH, D_MODEL, D_MODEL), D_MODEL ** -0.5),
        'peer_wq': nrm(28, (DEPTH, D_MODEL, PEER_HEADS * PEER_QDIM), D_MODEL ** -0.5),
        'peer_keys': nrm(29, (DEPTH, PEER_HEADS, 2, PEER_KEYS, PEER_HALF), PEER_HALF ** -0.5),
        'peer_u': nrm(30, (DEPTH, N_EXPERTS, D_MODEL), D_MODEL ** -0.5),
        'peer_v': nrm(31, (DEPTH, N_EXPERTS, D_MODEL), PEER_HEADS ** -0.5),
    }


def reference(x_prompt, x_sample, state_rglru_h, state_rglru_conv, cache_meta_k, cache_meta_v,
              cache_win_k, cache_win_v, state_pool, meta_tokens, norm_mix, norm_ffn, w_in, conv_w, conv_b,
              lru_wa, lru_ba, lru_wx, lru_bx, lru_lambda, q_norm, k_norm, attn_sinks, pool_w, pool_scale,
              gate_b, w_branch, w_out, peer_wq, peer_keys, peer_u, peer_v):
    B = x_prompt.shape[0]
    dt = x_prompt.dtype
    xp = jnp.concatenate([jnp.broadcast_to(meta_tokens.astype(dt)[None], (B, N_META, D_MODEL)), x_prompt], axis=1)
    xs = x_sample
    ph, pc, pmk, pmv, pwk, pwv, ppool = [], [], [], [], [], [], []
    sh, sc, sk, sv, spool = [], [], [], [], []
    for l in range(DEPTH):
        w = {'w_in': w_in[l], 'conv_w': conv_w[l], 'conv_b': conv_b[l], 'lru_wa': lru_wa[l], 'lru_ba': lru_ba[l],
             'lru_wx': lru_wx[l], 'lru_bx': lru_bx[l], 'lru_lambda': lru_lambda[l], 'q_norm': q_norm[l],
             'k_norm': k_norm[l], 'attn_sinks': attn_sinks[l], 'pool_w': pool_w[l], 'pool_scale': pool_scale[l],
             'gate_b': gate_b[l], 'w_branch': w_branch[l], 'w_out': w_out[l]}
        mo, conv_n, h_n, pool_n, k_p, v_p = token_mixers(
            rmsnorm(xp, norm_mix[l]), jnp.zeros((B, CONV_W - 1, D_RNN), dt), jnp.zeros((B, D_RNN), dt),
            jnp.zeros((B, 0, D_POOL), dt), attend_prompt, w)
        xp = xp + mo
        xp = xp + peer_ffn(rmsnorm(xp, norm_ffn[l]), peer_wq[l], peer_keys[l], peer_u[l], peer_v[l])
        ph.append(h_n)
        pc.append(conv_n)
        pmk.append(k_p[:, :N_META])
        pmv.append(v_p[:, :N_META])
        pwk.append(k_p[:, -WINDOW:])
        pwv.append(v_p[:, -WINDOW:])
        ppool.append(pool_n)
        attend_s = functools.partial(attend_sample, cache_meta_k[l], cache_meta_v[l], cache_win_k[l], cache_win_v[l])
        mo, conv_n, h_n, pool_n, k_s, v_s = token_mixers(
            rmsnorm(xs, norm_mix[l]), state_rglru_conv[l], state_rglru_h[l], state_pool[l], attend_s, w)
        xs = xs + mo
        xs = xs + peer_ffn(rmsnorm(xs, norm_ffn[l]), peer_wq[l], peer_keys[l], peer_u[l], peer_v[l])
        sh.append(h_n)
        sc.append(conv_n)
        sk.append(k_s)
        sv.append(v_s)
        spool.append(pool_n)
    y_prompt = xp[:, N_META:]
    y_sample = xs
    p_rglru_h = jnp.stack(ph)
    p_rglru_conv = jnp.stack(pc)
    p_meta_k = jnp.stack(pmk)
    p_meta_v = jnp.stack(pmv)
    p_win_k = jnp.stack(pwk)
    p_win_v = jnp.stack(pwv)
    p_pool = jnp.stack(ppool)
    s_rglru_h = jnp.stack(sh)
    s_rglru_conv = jnp.stack(sc)
    s_new_k = jnp.stack(sk)
    s_new_v = jnp.stack(sv)
    s_pool = jnp.stack(spool)
    return (y_prompt, y_sample, p_rglru_h, p_rglru_conv, p_meta_k, p_meta_v, p_win_k, p_win_v, p_pool,
            s_rglru_h, s_rglru_conv, s_new_k, s_new_v, s_pool)
```

```python
import functools
import math

import jax
import jax.numpy as jnp
import numpy as np
from jax import lax
from jax.experimental import pallas as pl
from jax.experimental.pallas import tpu as pltpu

D_MODEL = 1024
CHUNK = 64
N_META = 16
EPS = 1e-6
D_RNN = 1024
RNN_BLOCK = 64
CONV_W = 4
LRU_C = 8.0
HEAD_DIM = 64
N_HEADS = 16
N_KV = 4
GQA_GROUP = N_HEADS // N_KV
D_KV = N_KV * HEAD_DIM
WINDOW = 128
D_POOL = 1024
POOL_SIZES = (2, 4, 8, 16)
POOL_GROUP = 256
POOL_MAX = 16
N_BRANCH = 3
PEER_HEADS = 8
PEER_KEYS = 128
N_EXPERTS = PEER_KEYS * PEER_KEYS
PEER_TOPK = 16
NEG_INF = -1e30

LANES = 128
SUBLANES = 8
MXU_TILE = 256
VMEM_LIMIT = 56 * 1024 * 1024

PROMPT_PAD = CHUNK - N_META
SEQ_COLS = 2 * D_RNN + D_POOL + 2 * D_KV

BF = jnp.bfloat16
F32 = jnp.float32


def _rmsnorm(x, g):
    xf = x.astype(F32)
    return xf * lax.rsqrt(jnp.mean(xf * xf, axis=-1, keepdims=True) + EPS) * g


def _head_sumsq(x, ones_bd):
    sq = x * x
    hi = sq.astype(BF)
    lo = (sq - hi.astype(F32)).astype(BF)
    out = []
    for c in range(x.shape[1] // MXU_TILE):
        sl = slice(c * MXU_TILE, (c + 1) * MXU_TILE)
        out.append(jnp.dot(hi[:, sl], ones_bd, preferred_element_type=F32)
                   + jnp.dot(lo[:, sl], ones_bd, preferred_element_type=F32))
    return out[0] if len(out) == 1 else jnp.concatenate(out, axis=1)


def _head_rmsnorm(x, g, ones_bd):
    ms = _head_sumsq(x, ones_bd) * (1.0 / HEAD_DIM)
    return x * lax.rsqrt(ms + EPS) * g


def _shift_rows(x, d, fill, rows):
    return jnp.where(rows < d, fill, pltpu.roll(x, d, axis=0))


def _seq_kernel(x_ref, g_ref, w_ref, cw_ref, cb_ref, wa_ref, wx_ref, ba_ref, bx_ref, lam_ref,
                kn_ref, ones_ref, pw_ref, ps_ref, ch_ref, h0_ref, ph_ref,
                ya_ref, yc_ref, k_ref, v_ref, cs_ref, hl_ref, pst_ref,
                exr, exp_, hc, *, tt, pad, pool_hist):
    t = pl.program_id(1)

    @pl.when(t == 0)
    def _():
        exr[0:SUBLANES, :] = ch_ref[0]
        exp_[0:POOL_MAX, :] = ph_ref[0]
        hc[...] = jnp.broadcast_to(h0_ref[0], hc.shape)

    hb = _rmsnorm(x_ref[...], g_ref[...]).astype(BF)
    z = jnp.dot(hb, w_ref[...], preferred_element_type=F32)
    xr = z[:, 0:D_RNN]
    gr = z[:, D_RNN:2 * D_RNN]
    xp = z[:, 2 * D_RNN:2 * D_RNN + D_POOL]
    kk = z[:, 2 * D_RNN + D_POOL:2 * D_RNN + D_POOL + D_KV]
    vv = z[:, 2 * D_RNN + D_POOL + D_KV:]

    rows = lax.broadcasted_iota(jnp.int32, (tt, 1), 0)
    pos = rows + (t * tt - pad)

    exr[SUBLANES:SUBLANES + tt, :] = xr
    xc = cb_ref[...] + cw_ref[3:4, :] * xr
    for j in range(CONV_W - 1):
        xc = xc + cw_ref[j:j + 1, :] * exr[SUBLANES - (CONV_W - 1) + j:SUBLANES - (CONV_W - 1) + j + tt, :]
    exr[0:SUBLANES, :] = exr[tt:tt + SUBLANES, :]
    cs_ref[0] = exr[0:SUBLANES, :]

    ra, ia = [], []
    for c in range(D_RNN // MXU_TILE):
        xcb = xc[:, c * MXU_TILE:(c + 1) * MXU_TILE].astype(BF)
        ra.append(jnp.dot(xcb, wa_ref[c], preferred_element_type=F32))
        ia.append(jnp.dot(xcb, wx_ref[c], preferred_element_type=F32))
    r = jax.nn.sigmoid(jnp.concatenate(ra, axis=1) + ba_ref[...])
    i = jax.nn.sigmoid(jnp.concatenate(ia, axis=1) + bx_ref[...])
    nl = -lam_ref[...]
    softplus = jnp.maximum(nl, 0.0) + jnp.log1p(jnp.exp(-jnp.abs(nl)))
    log_a = (-LRU_C * softplus) * r
    a = jnp.exp(log_a)
    u = jnp.sqrt(1.0 - jnp.exp(2.0 * log_a)) * (i * xc)
    if pad:
        u = jnp.where(pos >= 0, u, 0.0)

    d = 1
    while d < tt:
        a_s = _shift_rows(a, d, 1.0, rows)
        u_s = _shift_rows(u, d, 0.0, rows)
        u = a * u_s + u
        a = a * a_s
        d *= 2
    h = a * hc[0:1, :] + u
    hc[...] = jnp.broadcast_to(h[tt - 1:tt, :], hc.shape)
    hl_ref[0] = h[tt - 1:tt, :]
    ya_ref[...] = (h * jax.nn.gelu(gr, approximate=True)).astype(ya_ref.dtype)

    k_ref[...] = _head_rmsnorm(kk, kn_ref[...], ones_ref[...])
    v_ref[...] = vv

    exp_[POOL_MAX:POOL_MAX + tt, :] = xp
    ext = exp_[...]
    yc = []
    for gi, size in enumerate(POOL_SIZES):
        sl = slice(gi * POOL_GROUP, (gi + 1) * POOL_GROUP)
        s = ext[:, sl]
        step = 1
        while step < size:
            s = s + pltpu.roll(s, step, axis=0)
            step *= 2
        cnt = jnp.clip(pos + (pool_hist + 1), 1, size).astype(F32)
        pooled = s[POOL_MAX:, :] / cnt - xp[:, sl]
        yc.append(jnp.dot(pooled.astype(BF), pw_ref[gi], preferred_element_type=F32))
    yc_ref[...] = (jnp.concatenate(yc, axis=1) * ps_ref[...]).astype(yc_ref.dtype)
    exp_[0:POOL_MAX, :] = exp_[tt:tt + POOL_MAX, :]
    pst_ref[0] = exp_[0:POOL_MAX, :]


def _seq_mixer(x, lw, conv_hist8, h0, pool_hist16, *, nb, tt, pad, pool_hist):
    n = x.shape[0]
    nt = n // (nb * tt)
    row = lambda b, t: (b * nt + t, 0)
    full2 = lambda b, t: (0, 0)
    full3 = lambda b, t: (0, 0, 0)
    per_b = lambda b, t: (b, 0, 0)
    kern = functools.partial(_seq_kernel, tt=tt, pad=pad, pool_hist=pool_hist)
    return pl.pallas_call(
        kern,
        grid=(nb, nt),
        in_specs=[
            pl.BlockSpec((tt, D_MODEL), row),
            pl.BlockSpec((1, D_MODEL), full2),
            pl.BlockSpec((D_MODEL, SEQ_COLS), full2),
            pl.BlockSpec((CONV_W, D_RNN), full2),
            pl.BlockSpec((1, D_RNN), full2),
            pl.BlockSpec((D_RNN // MXU_TILE, MXU_TILE, MXU_TILE), full3),
            pl.BlockSpec((D_RNN // MXU_TILE, MXU_TILE, MXU_TILE), full3),
            pl.BlockSpec((1, D_RNN), full2),
            pl.BlockSpec((1, D_RNN), full2),
            pl.BlockSpec((1, D_RNN), full2),
            pl.BlockSpec((1, D_KV), full2),
            pl.BlockSpec((MXU_TILE, MXU_TILE), full2),
            pl.BlockSpec((len(POOL_SIZES), POOL_GROUP, POOL_GROUP), full3),
            pl.BlockSpec((1, D_POOL), full2),
            pl.BlockSpec((1, SUBLANES, D_RNN), per_b),
            pl.BlockSpec((1, 1, D_RNN), per_b),
            pl.BlockSpec((1, POOL_MAX, D_POOL), per_b),
        ],
        out_specs=[
            pl.BlockSpec((tt, D_RNN), row),
            pl.BlockSpec((tt, D_POOL), row),
            pl.BlockSpec((tt, D_KV), row),
            pl.BlockSpec((tt, D_KV), row),
            pl.BlockSpec((1, SUBLANES, D_RNN), per_b),
            pl.BlockSpec((1, 1, D_RNN), per_b),
            pl.BlockSpec((1, POOL_MAX, D_POOL), per_b),
        ],
        out_shape=[
            jax.ShapeDtypeStruct((n, D_RNN), BF),
            jax.ShapeDtypeStruct((n, D_POOL), BF),
            jax.ShapeDtypeStruct((n, D_KV), F32),
            jax.ShapeDtypeStruct((n, D_KV), F32),
            jax.ShapeDtypeStruct((nb, SUBLANES, D_RNN), F32),
            jax.ShapeDtypeStruct((nb, 1, D_RNN), F32),
            jax.ShapeDtypeStruct((nb, POOL_MAX, D_POOL), F32),
        ],
        scratch_shapes=[
            pltpu.VMEM((tt + SUBLANES, D_RNN), F32),
            pltpu.VMEM((tt + POOL_MAX, D_POOL), F32),
            pltpu.VMEM((SUBLANES, D_RNN), F32),
        ],
        compiler_params=pltpu.CompilerParams(
            dimension_semantics=("arbitrary", "arbitrary"), vmem_limit_bytes=VMEM_LIMIT),
        name="seq_mixer",
    )(x, lw["norm_mix"], lw["w_seq"], lw["conv_w"], lw["conv_b"], lw["wa"], lw["wx"], lw["ba"], lw["bx"],
      lw["lam"], lw["k_norm"], lw["ones_bd"], lw["pool_w"], lw["pool_scale"], conv_hist8, h0, pool_hist16)


def _attend(q4, kcat, vcat, sink_col, col_valid):
    lg = lax.dot_general(q4, kcat, (((1,), (1,)), ((), ())), preferred_element_type=F32) * (HEAD_DIM ** -0.5)
    if col_valid is not None:
        lg = jnp.where(col_valid, lg, NEG_INF)
    m = jnp.maximum(jnp.max(lg, axis=-1, keepdims=True), sink_col)
    e = jnp.exp(lg - m)
    p = e / (jnp.sum(e, axis=-1, keepdims=True) + jnp.exp(sink_col - m))
    return jnp.dot(p.astype(BF), vcat, preferred_element_type=F32)


def _project_q(x_ref, g_ref, wq_ref, qn_ref, ones_ref):
    hb = _rmsnorm(x_ref[...], g_ref[...]).astype(BF)
    q = jnp.dot(hb, wq_ref[...], preferred_element_type=F32)
    return _head_rmsnorm(q, qn_ref[...], ones_ref[...]).astype(BF)


def _sink_column(sink_ref, kh, rows_per_head):
    r = lax.broadcasted_iota(jnp.int32, (GQA_GROUP * rows_per_head, 1), 0)
    col = jnp.full((GQA_GROUP * rows_per_head, 1), sink_ref[kh * GQA_GROUP], F32)
    for gq in range(1, GQA_GROUP):
        col = jnp.where(r >= gq * rows_per_head, sink_ref[kh * GQA_GROUP + gq], col)
    return col


def _attn_prompt_kernel(sink_ref, x_ref, g_ref, wq_ref, qn_ref, ones_ref,
                        kc_ref, kh1_ref, kh2_ref, km_ref, vc_ref, vh1_ref, vh2_ref, vm_ref,
                        yb_ref, *, cpt):
    g = pl.program_id(1)
    q = _project_q(x_ref, g_ref, wq_ref, qn_ref, ones_ref)
    kwin = jnp.concatenate([kh1_ref[...], kh2_ref[...], kc_ref[...]], axis=0).astype(BF)
    vwin = jnp.concatenate([vh1_ref[...], vh2_ref[...], vc_ref[...]], axis=0).astype(BF)
    kmeta = km_ref[PROMPT_PAD:CHUNK, :].astype(BF)
    vmeta = vm_ref[PROMPT_PAD:CHUNK, :].astype(BF)
    n_keys = N_META + 3 * CHUNK
    col = lax.broadcasted_iota(jnp.int32, (1, n_keys), 1)
    for j in range(cpt):
        c = g * cpt + j
        first = N_META + CHUNK * jnp.clip(3 - c, 0, 3)
        first = jnp.where(c >= 1, first, n_keys)
        valid = (col < N_META) | (col >= first)
        for kh in range(N_KV):
            hs = slice(kh * HEAD_DIM, (kh + 1) * HEAD_DIM)
            kcat = jnp.concatenate([kmeta[:, hs], kwin[j * CHUNK:(j + 3) * CHUNK, hs]], axis=0)
            vcat = jnp.concatenate([vmeta[:, hs], vwin[j * CHUNK:(j + 3) * CHUNK, hs]], axis=0)
            q4 = jnp.concatenate(
                [q[j * CHUNK:(j + 1) * CHUNK, (kh * GQA_GROUP + gq) * HEAD_DIM:(kh * GQA_GROUP + gq + 1) * HEAD_DIM]
                 for gq in range(GQA_GROUP)], axis=0)
            o = _attend(q4, kcat, vcat, _sink_column(sink_ref, kh, CHUNK), valid)
            for gq in range(GQA_GROUP):
                hq = kh * GQA_GROUP + gq
                yb_ref[j * CHUNK:(j + 1) * CHUNK, hq * HEAD_DIM:(hq + 1) * HEAD_DIM] = (
                    o[gq * CHUNK:(gq + 1) * CHUNK, :].astype(yb_ref.dtype))


def _attn_prompt(x, k, v, lw, *, nb, cpt):
    n = x.shape[0]
    nchunks = n // (nb * CHUNK)
    nt = nchunks // cpt
    tt = cpt * CHUNK
    row = lambda b, g: (b * nt + g, 0)
    full2 = lambda b, g: (0, 0)
    halo1 = lambda b, g: (b * nchunks + jnp.maximum(g * cpt - 2, 0), 0)
    halo2 = lambda b, g: (b * nchunks + jnp.maximum(g * cpt - 1, 0), 0)
    meta = lambda b, g: (b * nchunks, 0)
    kv_specs = [pl.BlockSpec((tt, D_KV), row), pl.BlockSpec((CHUNK, D_KV), halo1),
                pl.BlockSpec((CHUNK, D_KV), halo2), pl.BlockSpec((CHUNK, D_KV), meta)]
    return pl.pallas_call(
        functools.partial(_attn_prompt_kernel, cpt=cpt),
        grid=(nb, nt),
        in_specs=[
            pl.BlockSpec(memory_space=pltpu.SMEM),
            pl.BlockSpec((tt, D_MODEL), row),
            pl.BlockSpec((1, D_MODEL), full2),
            pl.BlockSpec((D_MODEL, D_MODEL), full2),
            pl.BlockSpec((1, D_MODEL), full2),
            pl.BlockSpec((MXU_TILE, MXU_TILE), full2),
        ] + kv_specs + kv_specs,
        out_specs=pl.BlockSpec((tt, D_MODEL), row),
        out_shape=jax.ShapeDtypeStruct((n, D_MODEL), BF),
        compiler_params=pltpu.CompilerParams(
            dimension_semantics=("arbitrary", "arbitrary"), vmem_limit_bytes=VMEM_LIMIT),
        name="attn_prompt",
    )(lw["sinks"], x, lw["norm_mix"], lw["w_q"], lw["q_norm"], lw["ones_bd"], k, k, k, k, v, v, v, v)


def _attn_sample_kernel(sink_ref, x_ref, g_ref, wq_ref, qn_ref, ones_ref,
                        kn_ref, kw_ref, km_ref, vn_ref, vw_ref, vm_ref, yb_ref, *, t_new):
    q = _project_q(x_ref, g_ref, wq_ref, qn_ref, ones_ref)
    kall = jnp.concatenate([km_ref[0], kw_ref[0], kn_ref[...]], axis=0).astype(BF)
    vall = jnp.concatenate([vm_ref[0], vw_ref[0], vn_ref[...]], axis=0).astype(BF)
    for kh in range(N_KV):
        hs = slice(kh * HEAD_DIM, (kh + 1) * HEAD_DIM)
        q4 = jnp.concatenate(
            [q[:, (kh * GQA_GROUP + gq) * HEAD_DIM:(kh * GQA_GROUP + gq + 1) * HEAD_DIM]
             for gq in range(GQA_GROUP)], axis=0)
        o = _attend(q4, kall[:, hs], vall[:, hs], _sink_column(sink_ref, kh, t_new), None)
        for gq in range(GQA_GROUP):
            hq = kh * GQA_GROUP + gq
            yb_ref[:, hq * HEAD_DIM:(hq + 1) * HEAD_DIM] = o[gq * t_new:(gq + 1) * t_new, :].astype(yb_ref.dtype)


def _attn_sample(x, k, v, meta_k, meta_v, win_k, win_v, lw, *, nb, t_new):
    n = x.shape[0]
    row = lambda b: (b, 0)
    full2 = lambda b: (0, 0)
    per_b = lambda b: (b, 0, 0)
    kv_specs = [pl.BlockSpec((t_new, D_KV), row), pl.BlockSpec((1, WINDOW, D_KV), per_b),
                pl.BlockSpec((1, N_META, D_KV), per_b)]
    return pl.pallas_call(
        functools.partial(_attn_sample_kernel, t_new=t_new),
        grid=(nb,),
        in_specs=[
            pl.BlockSpec(memory_space=pltpu.SMEM),
            pl.BlockSpec((t_new, D_MODEL), row),
            pl.BlockSpec((1, D_MODEL), full2),
            pl.BlockSpec((D_MODEL, D_MODEL), full2),
            pl.BlockSpec((1, D_MODEL), full2),
            pl.BlockSpec((MXU_TILE, MXU_TILE), full2),
        ] + kv_specs + kv_specs,
        out_specs=pl.BlockSpec((t_new, D_MODEL), row),
        out_shape=jax.ShapeDtypeStruct((n, D_MODEL), BF),
        compiler_params=pltpu.CompilerParams(
            dimension_semantics=("arbitrary",), vmem_limit_bytes=VMEM_LIMIT),
        name="attn_sample",
    )(lw["sinks"], x, lw["norm_mix"], lw["w_q"], lw["q_norm"], lw["ones_bd"], k, win_k, meta_k, v, win_v, meta_v)


def _merge_kernel(x_ref, ya_ref, yb_ref, yc_ref, g_ref, wg_ref, gb_ref, wb_ref, wo_ref, o_ref):
    x = x_ref[...]
    hb = _rmsnorm(x, g_ref[...]).astype(BF)
    merged = None
    for nbr, y_ref in enumerate((ya_ref, yb_ref, yc_ref)):
        gl = jnp.dot(hb, wg_ref[:, nbr * D_MODEL:(nbr + 1) * D_MODEL], preferred_element_type=F32)
        gate = jax.nn.sigmoid(gl + gb_ref[:, nbr * D_MODEL:(nbr + 1) * D_MODEL])
        term = gate * jnp.dot(y_ref[...], wb_ref[nbr], preferred_element_type=F32)
        merged = term if merged is None else merged + term
    o_ref[...] = x + jnp.dot(merged.astype(BF), wo_ref[...], preferred_element_type=F32)


def _merge(x, ya, yb, yc, lw, *, tm):
    n = x.shape[0]
    row = lambda i: (i, 0)
    full2 = lambda i: (0, 0)
    full3 = lambda i: (0, 0, 0)
    tok = pl.BlockSpec((tm, D_MODEL), row)
    return pl.pallas_call(
        _merge_kernel,
        grid=(n // tm,),
        in_specs=[tok, tok, tok, tok,
                  pl.BlockSpec((1, D_MODEL), full2),
                  pl.BlockSpec((D_MODEL, N_BRANCH * D_MODEL), full2),
                  pl.BlockSpec((1, N_BRANCH * D_MODEL), full2),
                  pl.BlockSpec((N_BRANCH, D_MODEL, D_MODEL), full3),
                  pl.BlockSpec((D_MODEL, D_MODEL), full2)],
        out_specs=tok,
        out_shape=jax.ShapeDtypeStruct((n, D_MODEL), F32),
        compiler_params=pltpu.CompilerParams(
            dimension_semantics=("arbitrary",), vmem_limit_bytes=VMEM_LIMIT),
        name="merge",
    )(x, ya, yb, yc, lw["norm_mix"], lw["w_g"], lw["gate_b"], lw["w_branch"], lw["w_out"])


def _top_values(s, k):
    vals = []
    cur = s
    for _ in range(k):
        m = jnp.max(cur, axis=0, keepdims=True)
        vals.append(m)
        cur = jnp.where(cur == m, -jnp.inf, cur)
    return jnp.concatenate(vals, axis=0)


def _kth_largest(s, k):
    cur = s
    left = jnp.full((1, s.shape[1]), float(k), F32)
    thr = jnp.full((1, s.shape[1]), jnp.inf, F32)
    for _ in range(k):
        m = jnp.max(cur, axis=0, keepdims=True)
        eq = cur == m
        thr = jnp.where(left > 0.0, m, thr)
        left = left - jnp.sum(eq.astype(F32), axis=0, keepdims=True)
        cur = jnp.where(eq, -jnp.inf, cur)
    return thr


def _peer_kernel(x_ref, g_ref, wqt_ref, keys_ref, u_ref, vt_ref, o_ref,
                 hb_s, s0_s, s1_s, e1_s, thr_s, m0_s, coef_s, acc_s, *, tm, ec):
    c = pl.program_id(1)
    nc = pl.num_programs(1)

    @pl.when(c == 0)
    def _():
        hb = _rmsnorm(x_ref[...], g_ref[...]).astype(BF)
        hb_s[...] = hb
        qt = lax.dot_general(wqt_ref[...], hb, (((1,), (1,)), ((), ())), preferred_element_type=F32).astype(BF)
        for h in range(PEER_HEADS):
            st = []
            for p in range(2):
                r0 = (h * 2 + p) * PEER_KEYS
                st.append(jnp.dot(keys_ref[h * 2 + p], qt[r0:r0 + PEER_KEYS, :], preferred_element_type=F32))
            t0 = _top_values(st[0], PEER_TOPK)
            t1 = _top_values(st[1], PEER_TOPK)
            cand = jnp.concatenate([t0[a:a + 1, :] + t1 for a in range(PEER_TOPK)], axis=0)
            thr = _kth_largest(cand, PEER_TOPK)
            m0 = t0[0:1, :]
            m1 = t1[0:1, :]
            zsum = jnp.sum(jnp.where(cand >= thr, jnp.exp(cand - (m0 + m1)), 0.0), axis=0, keepdims=True)
            rs = slice(h * PEER_KEYS, (h + 1) * PEER_KEYS)
            s0_s[rs, :] = st[0]
            s1_s[rs, :] = st[1]
            e1_s[rs, :] = jnp.exp(st[1] - m1) / zsum
            thr_s[h:h + 1, :] = thr
            m0_s[h:h + 1, :] = m0
        acc_s[...] = jnp.zeros_like(acc_s)

    hut = lax.dot_general(u_ref[...], hb_s[...], (((1,), (1,)), ((), ())), preferred_element_type=F32)
    sqrt_half = math.sqrt(0.5)
    nblk = ec // PEER_KEYS
    for il in range(nblk):
        for cb in range(tm // LANES):
            ls = slice(cb * LANES, (cb + 1) * LANES)
            gate = jnp.zeros((PEER_KEYS, LANES), F32)
            for h in range(PEER_HEADS):
                grp = pl.multiple_of(h * PEER_KEYS + c * nblk + (il // SUBLANES) * SUBLANES, SUBLANES)
                s0row = s0_s[pl.ds(grp, SUBLANES), ls][il % SUBLANES:il % SUBLANES + 1, :]
                a0 = jnp.exp(s0row - m0_s[h:h + 1, ls])
                rs = slice(h * PEER_KEYS, (h + 1) * PEER_KEYS)
                sel = (s0row + s1_s[rs, ls]) >= thr_s[h:h + 1, ls]
                gate = gate + jnp.where(sel, e1_s[rs, ls], 0.0) * a0
            hu = hut[il * PEER_KEYS:(il + 1) * PEER_KEYS, ls]
            act = 0.5 * hu * (1.0 + lax.erf(hu * sqrt_half))
            coef_s[il * PEER_KEYS:(il + 1) * PEER_KEYS, ls] = (gate * act).astype(BF)
    acc_s[...] += jnp.dot(vt_ref[...], coef_s[...], preferred_element_type=F32)

    @pl.when(c == nc - 1)
    def _():
        o_ref[...] = x_ref[...] + acc_s[...].T


def _peer(x, lw, *, tm, ec):
    n = x.shape[0]
    assert n % tm == 0 and tm % LANES == 0 and ec % (PEER_KEYS * SUBLANES) == 0
    tok = pl.BlockSpec((tm, D_MODEL), lambda i, c: (i, 0))
    nq = PEER_HEADS * 2 * PEER_KEYS
    return pl.pallas_call(
        functools.partial(_peer_kernel, tm=tm, ec=ec),
        grid=(n // tm, N_EXPERTS // ec),
        in_specs=[tok,
                  pl.BlockSpec((1, D_MODEL), lambda i, c: (0, 0)),
                  pl.BlockSpec((nq, D_MODEL), lambda i, c: (0, 0)),
                  pl.BlockSpec((PEER_HEADS * 2, PEER_KEYS, PEER_KEYS), lambda i, c: (0, 0, 0)),
                  pl.BlockSpec((ec, D_MODEL), lambda i, c: (c, 0)),
                  pl.BlockSpec((D_MODEL, ec), lambda i, c: (0, c))],
        out_specs=tok,
        out_shape=jax.ShapeDtypeStruct((n, D_MODEL), F32),
        scratch_shapes=[
            pltpu.VMEM((tm, D_MODEL), BF),
            pltpu.VMEM((PEER_HEADS * PEER_KEYS, tm), F32),
            pltpu.VMEM((PEER_HEADS * PEER_KEYS, tm), F32),
            pltpu.VMEM((PEER_HEADS * PEER_KEYS, tm), F32),
            pltpu.VMEM((PEER_HEADS, tm), F32),
            pltpu.VMEM((PEER_HEADS, tm), F32),
            pltpu.VMEM((ec, tm), BF),
            pltpu.VMEM((D_MODEL, tm), F32),
        ],
        compiler_params=pltpu.CompilerParams(
            dimension_semantics=("arbitrary", "arbitrary"), vmem_limit_bytes=VMEM_LIMIT),
        name="peer",
    )(x, lw["norm_ffn"], lw["wq_t"], lw["keys"], lw["u"], lw["v_t"])


def _block_diag_pack(w):
    per = MXU_TILE // RNN_BLOCK
    nsup = w.shape[0] // per
    out = jnp.zeros((nsup, MXU_TILE, MXU_TILE), w.dtype)
    for s in range(nsup):
        for j in range(per):
            out = out.at[s, j * RNN_BLOCK:(j + 1) * RNN_BLOCK, j * RNN_BLOCK:(j + 1) * RNN_BLOCK].set(w[s * per + j])
    return out


def _layer_weights(l, norm_mix, norm_ffn, w_in, conv_w, conv_b, lru_wa, lru_ba, lru_wx, lru_bx, lru_lambda,
                   q_norm, k_norm, attn_sinks, pool_w, pool_scale, gate_b, w_branch, w_out,
                   peer_wq, peer_keys, peer_u, peer_v):
    w = w_in[l]
    o_q = 2 * D_RNN
    o_k = o_q + D_MODEL
    o_v = o_k + D_KV
    o_p = o_v + D_KV
    o_g = o_p + D_POOL
    ones_bd = jnp.asarray(np.kron(np.eye(MXU_TILE // HEAD_DIM), np.ones((HEAD_DIM, HEAD_DIM))), BF)
    return {
        "norm_mix": norm_mix[l][None, :],
        "norm_ffn": norm_ffn[l][None, :],
        "w_seq": jnp.concatenate([w[:, :o_q], w[:, o_p:o_g], w[:, o_k:o_p]], axis=1).astype(BF),
        "w_q": w[:, o_q:o_k].astype(BF),
        "w_g": w[:, o_g:].astype(BF),
        "conv_w": conv_w[l],
        "conv_b": conv_b[l][None, :],
        "wa": _block_diag_pack(lru_wa[l]).astype(BF),
        "wx": _block_diag_pack(lru_wx[l]).astype(BF),
        "ba": lru_ba[l][None, :],
        "bx": lru_bx[l][None, :],
        "lam": lru_lambda[l][None, :],
        "q_norm": jnp.tile(q_norm[l], N_HEADS)[None, :],
        "k_norm": jnp.tile(k_norm[l], N_KV)[None, :],
        "ones_bd": ones_bd,
        "sinks": attn_sinks[l],
        "pool_w": pool_w[l].astype(BF),
        "pool_scale": pool_scale[l][None, :],
        "gate_b": gate_b[l].reshape(1, N_BRANCH * D_MODEL),
        "w_branch": w_branch[l].astype(BF),
        "w_out": w_out[l].astype(BF),
        "wq_t": peer_wq[l].T.astype(BF),
        "keys": peer_keys[l].reshape(PEER_HEADS * 2, PEER_KEYS, PEER_KEYS).astype(BF),
        "u": peer_u[l].astype(BF),
        "v_t": peer_v[l].T.astype(BF),
    }


def _pick_tile(n, candidates):
    for t in candidates:
        if n % t == 0:
            return t
    raise ValueError(f"no tile for {n}")


def kernel(x_prompt, x_sample, state_rglru_h, state_rglru_conv, cache_meta_k, cache_meta_v, cache_win_k, cache_win_v, state_pool, meta_tokens, norm_mix, norm_ffn, w_in, conv_w, conv_b, lru_wa, lru_ba, lru_wx, lru_bx, lru_lambda, q_norm, k_norm, attn_sinks, pool_w, pool_scale, gate_b, w_branch, w_out, peer_wq, peer_keys, peer_u, peer_v):
    bp, seq, _ = x_prompt.shape
    bs, ts, _ = x_sample.shape
    depth = w_in.shape[0]
    dt = x_prompt.dtype
    s_pad = PROMPT_PAD + N_META + seq
    nchunks = s_pad // CHUNK
    cpt = 3 if nchunks % 3 == 0 else 1
    tt_p = cpt * CHUNK

    xp = jnp.concatenate([jnp.zeros((bp, PROMPT_PAD, D_MODEL), dt),
                          jnp.broadcast_to(meta_tokens.astype(dt)[None], (bp, N_META, D_MODEL)), x_prompt], axis=1)
    xp = xp.reshape(bp * s_pad, D_MODEL)
    xs = x_sample.reshape(bs * ts, D_MODEL)
    tm_p = _pick_tile(bp * s_pad, (384, 256, 128))
    tm_s = _pick_tile(bs * ts, (256, 128))

    zeros_ch = jnp.zeros((bp, SUBLANES, D_RNN), dt)
    zeros_h = jnp.zeros((bp, 1, D_RNN), dt)
    zeros_ph = jnp.zeros((bp, POOL_MAX, D_POOL), dt)

    outs = {k: [] for k in ("ph", "pc", "pmk", "pmv", "pwk", "pwv", "ppool", "sh", "sc", "sk", "sv", "spool")}
    for l in range(depth):
        lw = _layer_weights(l, norm_mix, norm_ffn, w_in, conv_w, conv_b, lru_wa, lru_ba, lru_wx, lru_bx,
                            lru_lambda, q_norm, k_norm, attn_sinks, pool_w, pool_scale, gate_b, w_branch,
                            w_out, peer_wq, peer_keys, peer_u, peer_v)
        ya, yc, k, v, cs, hl, pst = _seq_mixer(xp, lw, zeros_ch, zeros_h, zeros_ph,
                                               nb=bp, tt=tt_p, pad=PROMPT_PAD, pool_hist=0)
        yb = _attn_prompt(xp, k, v, lw, nb=bp, cpt=cpt)
        xp = _merge(xp, ya, yb, yc, lw, tm=tm_p)
        xp = _peer(xp, lw, tm=tm_p, ec=1024)
        k4 = k.reshape(bp, s_pad, N_KV, HEAD_DIM)
        v4 = v.reshape(bp, s_pad, N_KV, HEAD_DIM)
        outs["ph"].append(hl[:, 0])
        outs["pc"].append(cs[:, SUBLANES - (CONV_W - 1):])
        outs["pmk"].append(k4[:, PROMPT_PAD:PROMPT_PAD + N_META])
        outs["pmv"].append(v4[:, PROMPT_PAD:PROMPT_PAD + N_META])
        outs["pwk"].append(k4[:, -WINDOW:])
        outs["pwv"].append(v4[:, -WINDOW:])
        outs["ppool"].append(pst[:, 1:])
        ch8 = jnp.pad(state_rglru_conv[l], ((0, 0), (SUBLANES - (CONV_W - 1), 0), (0, 0)))
        ph16 = jnp.pad(state_pool[l], ((0, 0), (1, 0), (0, 0)))
        ya, yc, k, v, cs, hl, pst = _seq_mixer(xs, lw, ch8, state_rglru_h[l][:, None, :], ph16,
                                               nb=bs, tt=ts, pad=0, pool_hist=POOL_MAX - 1)
        yb = _attn_sample(xs, k, v, cache_meta_k[l].reshape(bs, N_META, D_KV),
                          cache_meta_v[l].reshape(bs, N_META, D_KV),
                          cache_win_k[l].reshape(bs, WINDOW, D_KV), cache_win_v[l].reshape(bs, WINDOW, D_KV),
                          lw, nb=bs, t_new=ts)
        xs = _merge(xs, ya, yb, yc, lw, tm=tm_s)
        xs = _peer(xs, lw, tm=tm_s, ec=1024)
        outs["sh"].append(hl[:, 0])
        outs["sc"].append(cs[:, SUBLANES - (CONV_W - 1):])
        outs["sk"].append(k.reshape(bs, ts, N_KV, HEAD_DIM))
        outs["sv"].append(v.reshape(bs, ts, N_KV, HEAD_DIM))
        outs["spool"].append(pst[:, 1:])

    y_prompt = xp.reshape(bp, s_pad, D_MODEL)[:, PROMPT_PAD + N_META:]
    y_sample = xs.reshape(bs, ts, D_MODEL)
    st = lambda key: jnp.stack(outs[key])
    return (y_prompt, y_sample, st("ph"), st("pc"), st("pmk"), st("pmv"), st("pwk"), st("pwv"), st("ppool"),
            st("sh"), st("sc"), st("sk"), st("sv"), st("spool"))
```

```python
import functools
import math

import jax
import jax.numpy as jnp
import numpy as np
from jax import lax
from jax.experimental import pallas as pl
from jax.experimental.pallas import tpu as pltpu

D_MODEL = 1024
CHUNK = 64
N_META = 16
EPS = 1e-6
D_RNN = 1024
RNN_BLOCK = 64
CONV_W = 4
LRU_C = 8.0
HEAD_DIM = 64
N_HEADS = 16
N_KV = 4
GQA_GROUP = N_HEADS // N_KV
D_KV = N_KV * HEAD_DIM
WINDOW = 128
D_POOL = 1024
POOL_SIZES = (2, 4, 8, 16)
POOL_GROUP = 256
POOL_MAX = 16
N_BRANCH = 3
PEER_HEADS = 8
PEER_KEYS = 128
N_EXPERTS = PEER_KEYS * PEER_KEYS
PEER_TOPK = 16
PEER_PIECE = 512
NEG_INF = -1e30

LANES = 128
SUBLANES = 8
MXU_TILE = 256
VMEM_LIMIT = 56 * 1024 * 1024

PROMPT_PAD = CHUNK - N_META
SEQ_COLS = 2 * D_RNN + D_POOL + 2 * D_KV

BF = jnp.bfloat16
F32 = jnp.float32


def _rmsnorm(x, g):
    xf = x.astype(F32)
    return xf * lax.rsqrt(jnp.mean(xf * xf, axis=-1, keepdims=True) + EPS) * g


def _head_sumsq(x, ones_bd):
    sq = x * x
    hi = sq.astype(BF)
    lo = (sq - hi.astype(F32)).astype(BF)
    out = []
    for c in range(x.shape[1] // MXU_TILE):
        sl = slice(c * MXU_TILE, (c + 1) * MXU_TILE)
        out.append(jnp.dot(hi[:, sl], ones_bd, preferred_element_type=F32)
                   + jnp.dot(lo[:, sl], ones_bd, preferred_element_type=F32))
    return out[0] if len(out) == 1 else jnp.concatenate(out, axis=1)


def _head_rmsnorm(x, g, ones_bd):
    ms = _head_sumsq(x, ones_bd) * (1.0 / HEAD_DIM)
    return x * lax.rsqrt(ms + EPS) * g


def _shift_rows(x, d, fill, rows):
    return jnp.where(rows < d, fill, pltpu.roll(x, d, axis=0))


def _seq_kernel(x_ref, g_ref, w_ref, cw_ref, cb_ref, wa_ref, wx_ref, ba_ref, bx_ref, lam_ref,
                kn_ref, ones_ref, pw_ref, ps_ref, ch_ref, h0_ref, ph_ref,
                ya_ref, yc_ref, k_ref, v_ref, cs_ref, hl_ref, pst_ref,
                exr, exp_, hc, *, tt, pad, pool_hist):
    t = pl.program_id(1)

    @pl.when(t == 0)
    def _():
        exr[0:SUBLANES, :] = ch_ref[0]
        exp_[0:POOL_MAX, :] = ph_ref[0]
        hc[...] = jnp.broadcast_to(h0_ref[0], hc.shape)

    hb = _rmsnorm(x_ref[...], g_ref[...]).astype(BF)
    z = jnp.dot(hb, w_ref[...], preferred_element_type=F32)
    xr = z[:, 0:D_RNN]
    gr = z[:, D_RNN:2 * D_RNN]
    xp = z[:, 2 * D_RNN:2 * D_RNN + D_POOL]
    kk = z[:, 2 * D_RNN + D_POOL:2 * D_RNN + D_POOL + D_KV]
    vv = z[:, 2 * D_RNN + D_POOL + D_KV:]

    rows = lax.broadcasted_iota(jnp.int32, (tt, 1), 0)
    pos = rows + (t * tt - pad)

    exr[SUBLANES:SUBLANES + tt, :] = xr
    xc = cb_ref[...] + cw_ref[3:4, :] * xr
    for j in range(CONV_W - 1):
        xc = xc + cw_ref[j:j + 1, :] * exr[SUBLANES - (CONV_W - 1) + j:SUBLANES - (CONV_W - 1) + j + tt, :]
    exr[0:SUBLANES, :] = exr[tt:tt + SUBLANES, :]
    cs_ref[0] = exr[0:SUBLANES, :]

    ra, ia = [], []
    for c in range(D_RNN // MXU_TILE):
        xcb = xc[:, c * MXU_TILE:(c + 1) * MXU_TILE].astype(BF)
        ra.append(jnp.dot(xcb, wa_ref[c], preferred_element_type=F32))
        ia.append(jnp.dot(xcb, wx_ref[c], preferred_element_type=F32))
    r = jax.nn.sigmoid(jnp.concatenate(ra, axis=1) + ba_ref[...])
    i = jax.nn.sigmoid(jnp.concatenate(ia, axis=1) + bx_ref[...])
    nl = -lam_ref[...]
    softplus = jnp.maximum(nl, 0.0) + jnp.log1p(jnp.exp(-jnp.abs(nl)))
    log_a = (-LRU_C * softplus) * r
    a = jnp.exp(log_a)
    u = jnp.sqrt(1.0 - jnp.exp(2.0 * log_a)) * (i * xc)
    if pad:
        u = jnp.where(pos >= 0, u, 0.0)

    d = 1
    while d < tt:
        a_s = _shift_rows(a, d, 1.0, rows)
        u_s = _shift_rows(u, d, 0.0, rows)
        u = a * u_s + u
        a = a * a_s
        d *= 2
    h = a * hc[0:1, :] + u
    hc[...] = jnp.broadcast_to(h[tt - 1:tt, :], hc.shape)
    hl_ref[0] = h[tt - 1:tt, :]
    ya_ref[...] = (h * jax.nn.gelu(gr, approximate=True)).astype(ya_ref.dtype)

    k_ref[...] = _head_rmsnorm(kk, kn_ref[...], ones_ref[...])
    v_ref[...] = vv

    exp_[POOL_MAX:POOL_MAX + tt, :] = xp
    ext = exp_[...]
    yc = []
    for gi, size in enumerate(POOL_SIZES):
        sl = slice(gi * POOL_GROUP, (gi + 1) * POOL_GROUP)
        s = ext[:, sl]
        step = 1
        while step < size:
            s = s + pltpu.roll(s, step, axis=0)
            step *= 2
        cnt = jnp.clip(pos + (pool_hist + 1), 1, size).astype(F32)
        pooled = s[POOL_MAX:, :] / cnt - xp[:, sl]
        yc.append(jnp.dot(pooled.astype(BF), pw_ref[gi], preferred_element_type=F32))
    yc_ref[...] = (jnp.concatenate(yc, axis=1) * ps_ref[...]).astype(yc_ref.dtype)
    exp_[0:POOL_MAX, :] = exp_[tt:tt + POOL_MAX, :]
    pst_ref[0] = exp_[0:POOL_MAX, :]


def _seq_mixer(x, lw, conv_hist8, h0, pool_hist16, *, nb, nt, tt, row0, pad, pool_hist):
    n = nb * nt * tt
    assert row0 % tt == 0
    xrow = lambda b, t: (row0 // tt + b * nt + t, 0)
    row = lambda b, t: (b * nt + t, 0)
    full2 = lambda b, t: (0, 0)
    full3 = lambda b, t: (0, 0, 0)
    per_b = lambda b, t: (b, 0, 0)
    kern = functools.partial(_seq_kernel, tt=tt, pad=pad, pool_hist=pool_hist)
    return pl.pallas_call(
        kern,
        grid=(nb, nt),
        in_specs=[
            pl.BlockSpec((tt, D_MODEL), xrow),
            pl.BlockSpec((1, D_MODEL), full2),
            pl.BlockSpec((D_MODEL, SEQ_COLS), full2),
            pl.BlockSpec((CONV_W, D_RNN), full2),
            pl.BlockSpec((1, D_RNN), full2),
            pl.BlockSpec((D_RNN // MXU_TILE, MXU_TILE, MXU_TILE), full3),
            pl.BlockSpec((D_RNN // MXU_TILE, MXU_TILE, MXU_TILE), full3),
            pl.BlockSpec((1, D_RNN), full2),
            pl.BlockSpec((1, D_RNN), full2),
            pl.BlockSpec((1, D_RNN), full2),
            pl.BlockSpec((1, D_KV), full2),
            pl.BlockSpec((MXU_TILE, MXU_TILE), full2),
            pl.BlockSpec((len(POOL_SIZES), POOL_GROUP, POOL_GROUP), full3),
            pl.BlockSpec((1, D_POOL), full2),
            pl.BlockSpec((1, SUBLANES, D_RNN), per_b),
            pl.BlockSpec((1, 1, D_RNN), per_b),
            pl.BlockSpec((1, POOL_MAX, D_POOL), per_b),
        ],
        out_specs=[
            pl.BlockSpec((tt, D_RNN), row),
            pl.BlockSpec((tt, D_POOL), row),
            pl.BlockSpec((tt, D_KV), row),
            pl.BlockSpec((tt, D_KV), row),
            pl.BlockSpec((1, SUBLANES, D_RNN), per_b),
            pl.BlockSpec((1, 1, D_RNN), per_b),
            pl.BlockSpec((1, POOL_MAX, D_POOL), per_b),
        ],
        out_shape=[
            jax.ShapeDtypeStruct((n, D_RNN), BF),
            jax.ShapeDtypeStruct((n, D_POOL), BF),
            jax.ShapeDtypeStruct((n, D_KV), F32),
            jax.ShapeDtypeStruct((n, D_KV), F32),
            jax.ShapeDtypeStruct((nb, SUBLANES, D_RNN), F32),
            jax.ShapeDtypeStruct((nb, 1, D_RNN), F32),
            jax.ShapeDtypeStruct((nb, POOL_MAX, D_POOL), F32),
        ],
        scratch_shapes=[
            pltpu.VMEM((tt + SUBLANES, D_RNN), F32),
            pltpu.VMEM((tt + POOL_MAX, D_POOL), F32),
            pltpu.VMEM((SUBLANES, D_RNN), F32),
        ],
        compiler_params=pltpu.CompilerParams(
            dimension_semantics=("arbitrary", "arbitrary"), vmem_limit_bytes=VMEM_LIMIT),
        name="seq_mixer",
    )(x, lw["norm_mix"], lw["w_seq"], lw["conv_w"], lw["conv_b"], lw["wa"], lw["wx"], lw["ba"], lw["bx"],
      lw["lam"], lw["k_norm"], lw["ones_bd"], lw["pool_w"], lw["pool_scale"], conv_hist8, h0, pool_hist16)


def _attend(q4, kcat, vcat, sink_col, col_valid):
    lg = lax.dot_general(q4, kcat, (((1,), (1,)), ((), ())), preferred_element_type=F32) * (HEAD_DIM ** -0.5)
    if col_valid is not None:
        lg = jnp.where(col_valid, lg, NEG_INF)
    m = jnp.maximum(jnp.max(lg, axis=-1, keepdims=True), sink_col)
    e = jnp.exp(lg - m)
    p = e / (jnp.sum(e, axis=-1, keepdims=True) + jnp.exp(sink_col - m))
    return jnp.dot(p.astype(BF), vcat, preferred_element_type=F32)


def _project_q(x_ref, g_ref, wq_ref, qn_ref, ones_ref):
    hb = _rmsnorm(x_ref[...], g_ref[...]).astype(BF)
    q = jnp.dot(hb, wq_ref[...], preferred_element_type=F32)
    return _head_rmsnorm(q, qn_ref[...], ones_ref[...]).astype(BF)


def _sink_column(sink_ref, kh, rows_per_head):
    r = lax.broadcasted_iota(jnp.int32, (GQA_GROUP * rows_per_head, 1), 0)
    col = jnp.full((GQA_GROUP * rows_per_head, 1), sink_ref[kh * GQA_GROUP], F32)
    for gq in range(1, GQA_GROUP):
        col = jnp.where(r >= gq * rows_per_head, sink_ref[kh * GQA_GROUP + gq], col)
    return col


def _attn_prompt_kernel(sink_ref, x_ref, g_ref, wq_ref, qn_ref, ones_ref,
                        kc_ref, kh1_ref, kh2_ref, km_ref, vc_ref, vh1_ref, vh2_ref, vm_ref,
                        yb_ref, *, cpt):
    g = pl.program_id(1)
    q = _project_q(x_ref, g_ref, wq_ref, qn_ref, ones_ref)
    kwin = jnp.concatenate([kh1_ref[...], kh2_ref[...], kc_ref[...]], axis=0).astype(BF)
    vwin = jnp.concatenate([vh1_ref[...], vh2_ref[...], vc_ref[...]], axis=0).astype(BF)
    kmeta = km_ref[PROMPT_PAD:CHUNK, :].astype(BF)
    vmeta = vm_ref[PROMPT_PAD:CHUNK, :].astype(BF)
    n_keys = N_META + 3 * CHUNK
    col = lax.broadcasted_iota(jnp.int32, (1, n_keys), 1)
    for j in range(cpt):
        c = g * cpt + j
        first = N_META + CHUNK * jnp.clip(3 - c, 0, 3)
        first = jnp.where(c >= 1, first, n_keys)
        valid = (col < N_META) | (col >= first)
        for kh in range(N_KV):
            hs = slice(kh * HEAD_DIM, (kh + 1) * HEAD_DIM)
            kcat = jnp.concatenate([kmeta[:, hs], kwin[j * CHUNK:(j + 3) * CHUNK, hs]], axis=0)
            vcat = jnp.concatenate([vmeta[:, hs], vwin[j * CHUNK:(j + 3) * CHUNK, hs]], axis=0)
            q4 = jnp.concatenate(
                [q[j * CHUNK:(j + 1) * CHUNK, (kh * GQA_GROUP + gq) * HEAD_DIM:(kh * GQA_GROUP + gq + 1) * HEAD_DIM]
                 for gq in range(GQA_GROUP)], axis=0)
            o = _attend(q4, kcat, vcat, _sink_column(sink_ref, kh, CHUNK), valid)
            for gq in range(GQA_GROUP):
                hq = kh * GQA_GROUP + gq
                yb_ref[j * CHUNK:(j + 1) * CHUNK, hq * HEAD_DIM:(hq + 1) * HEAD_DIM] = (
                    o[gq * CHUNK:(gq + 1) * CHUNK, :].astype(yb_ref.dtype))


def _attn_prompt(x, k, v, lw, *, nb, cpt):
    n = k.shape[0]
    nchunks = n // (nb * CHUNK)
    nt = nchunks // cpt
    tt = cpt * CHUNK
    row = lambda b, g: (b * nt + g, 0)
    full2 = lambda b, g: (0, 0)
    halo1 = lambda b, g: (b * nchunks + jnp.maximum(g * cpt - 2, 0), 0)
    halo2 = lambda b, g: (b * nchunks + jnp.maximum(g * cpt - 1, 0), 0)
    meta = lambda b, g: (b * nchunks, 0)
    kv_specs = [pl.BlockSpec((tt, D_KV), row), pl.BlockSpec((CHUNK, D_KV), halo1),
                pl.BlockSpec((CHUNK, D_KV), halo2), pl.BlockSpec((CHUNK, D_KV), meta)]
    return pl.pallas_call(
        functools.partial(_attn_prompt_kernel, cpt=cpt),
        grid=(nb, nt),
        in_specs=[
            pl.BlockSpec(memory_space=pltpu.SMEM),
            pl.BlockSpec((tt, D_MODEL), row),
            pl.BlockSpec((1, D_MODEL), full2),
            pl.BlockSpec((D_MODEL, D_MODEL), full2),
            pl.BlockSpec((1, D_MODEL), full2),
            pl.BlockSpec((MXU_TILE, MXU_TILE), full2),
        ] + kv_specs + kv_specs,
        out_specs=pl.BlockSpec((tt, D_MODEL), row),
        out_shape=jax.ShapeDtypeStruct((n, D_MODEL), BF),
        compiler_params=pltpu.CompilerParams(
            dimension_semantics=("arbitrary", "arbitrary"), vmem_limit_bytes=VMEM_LIMIT),
        name="attn_prompt",
    )(lw["sinks"], x, lw["norm_mix"], lw["w_q"], lw["q_norm"], lw["ones_bd"], k, k, k, k, v, v, v, v)


def _attn_sample_kernel(sink_ref, x_ref, g_ref, wq_ref, qn_ref, ones_ref,
                        kn_ref, kw_ref, km_ref, vn_ref, vw_ref, vm_ref, yb_ref, *, t_new):
    q = _project_q(x_ref, g_ref, wq_ref, qn_ref, ones_ref)
    kall = jnp.concatenate([km_ref[0], kw_ref[0], kn_ref[...]], axis=0).astype(BF)
    vall = jnp.concatenate([vm_ref[0], vw_ref[0], vn_ref[...]], axis=0).astype(BF)
    for kh in range(N_KV):
        hs = slice(kh * HEAD_DIM, (kh + 1) * HEAD_DIM)
        q4 = jnp.concatenate(
            [q[:, (kh * GQA_GROUP + gq) * HEAD_DIM:(kh * GQA_GROUP + gq + 1) * HEAD_DIM]
             for gq in range(GQA_GROUP)], axis=0)
        o = _attend(q4, kall[:, hs], vall[:, hs], _sink_column(sink_ref, kh, t_new), None)
        for gq in range(GQA_GROUP):
            hq = kh * GQA_GROUP + gq
            yb_ref[:, hq * HEAD_DIM:(hq + 1) * HEAD_DIM] = o[gq * t_new:(gq + 1) * t_new, :].astype(yb_ref.dtype)


def _attn_sample(x, k, v, meta_k, meta_v, win_k, win_v, lw, *, nb, t_new, row0):
    n = nb * t_new
    assert row0 % t_new == 0
    xrow = lambda b: (row0 // t_new + b, 0)
    row = lambda b: (b, 0)
    full2 = lambda b: (0, 0)
    per_b = lambda b: (b, 0, 0)
    kv_specs = [pl.BlockSpec((t_new, D_KV), row), pl.BlockSpec((1, WINDOW, D_KV), per_b),
                pl.BlockSpec((1, N_META, D_KV), per_b)]
    return pl.pallas_call(
        functools.partial(_attn_sample_kernel, t_new=t_new),
        grid=(nb,),
        in_specs=[
            pl.BlockSpec(memory_space=pltpu.SMEM),
            pl.BlockSpec((t_new, D_MODEL), xrow),
            pl.BlockSpec((1, D_MODEL), full2),
            pl.BlockSpec((D_MODEL, D_MODEL), full2),
            pl.BlockSpec((1, D_MODEL), full2),
            pl.BlockSpec((MXU_TILE, MXU_TILE), full2),
        ] + kv_specs + kv_specs,
        out_specs=pl.BlockSpec((t_new, D_MODEL), row),
        out_shape=jax.ShapeDtypeStruct((n, D_MODEL), BF),
        compiler_params=pltpu.CompilerParams(
            dimension_semantics=("arbitrary",), vmem_limit_bytes=VMEM_LIMIT),
        name="attn_sample",
    )(lw["sinks"], x, lw["norm_mix"], lw["w_q"], lw["q_norm"], lw["ones_bd"], k, win_k, meta_k, v, win_v, meta_v)


def _merge_kernel(x_ref, *refs):
    y_refs, (g_ref, wg_ref, gb_ref, wb_ref, wo_ref, o_ref) = refs[:3 * N_BRANCH], refs[3 * N_BRANCH:]
    last = pl.program_id(0) == pl.num_programs(0) - 1
    x = x_ref[...]
    hb = _rmsnorm(x, g_ref[...]).astype(BF)
    merged = None
    for nbr in range(N_BRANCH):
        lo_ref, hi_ref, s_ref = y_refs[3 * nbr:3 * nbr + 3]
        y = jnp.concatenate([lo_ref[...], jnp.where(last, s_ref[...], hi_ref[...])], axis=0)
        gl = jnp.dot(hb, wg_ref[:, nbr * D_MODEL:(nbr + 1) * D_MODEL], preferred_element_type=F32)
        gate = jax.nn.sigmoid(gl + gb_ref[:, nbr * D_MODEL:(nbr + 1) * D_MODEL])
        term = gate * jnp.dot(y, wb_ref[nbr], preferred_element_type=F32)
        merged = term if merged is None else merged + term
    o_ref[...] = x + jnp.dot(merged.astype(BF), wo_ref[...], preferred_element_type=F32)


def _merge(x, ys_prompt, ys_sample, lw, *, tm):
    n = x.shape[0]
    half = tm // 2
    n_p = ys_prompt[0].shape[0]
    assert n % tm == 0 and n_p % tm == half and ys_sample[0].shape[0] == half
    last_half = n_p // half - 1
    row = lambda i: (i, 0)
    lo = lambda i: (jnp.minimum(2 * i, last_half), 0)
    hi = lambda i: (jnp.minimum(2 * i + 1, last_half), 0)
    full2 = lambda i: (0, 0)
    full3 = lambda i: (0, 0, 0)
    tok = pl.BlockSpec((tm, D_MODEL), row)
    y_specs, y_args = [], []
    for yp, ys in zip(ys_prompt, ys_sample):
        y_specs += [pl.BlockSpec((half, D_MODEL), lo), pl.BlockSpec((half, D_MODEL), hi),
                    pl.BlockSpec((half, D_MODEL), full2)]
        y_args += [yp, yp, ys]
    return pl.pallas_call(
        _merge_kernel,
        grid=(n // tm,),
        in_specs=[tok] + y_specs + [
                  pl.BlockSpec((1, D_MODEL), full2),
                  pl.BlockSpec((D_MODEL, N_BRANCH * D_MODEL), full2),
                  pl.BlockSpec((1, N_BRANCH * D_MODEL), full2),
                  pl.BlockSpec((N_BRANCH, D_MODEL, D_MODEL), full3),
                  pl.BlockSpec((D_MODEL, D_MODEL), full2)],
        out_specs=tok,
        out_shape=jax.ShapeDtypeStruct((n, D_MODEL), F32),
        compiler_params=pltpu.CompilerParams(
            dimension_semantics=("arbitrary",), vmem_limit_bytes=VMEM_LIMIT),
        name="merge",
    )(x, *y_args, lw["norm_mix"], lw["w_g"], lw["gate_b"], lw["w_branch"], lw["w_out"])


def _top_values(s, k, with_rank=False):
    vals = []
    cur = s
    rank = jnp.full(s.shape, float(k), F32)
    for i in range(k):
        m = jnp.max(cur, axis=0, keepdims=True)
        vals.append(m)
        eq = cur == m
        if with_rank:
            rank = jnp.where(eq, float(i), rank)
        cur = jnp.where(eq, -jnp.inf, cur)
    top = jnp.concatenate(vals, axis=0)
    return (top, rank) if with_rank else top


def _kth_largest(s, k):
    cur = s
    left = jnp.full((1, s.shape[1]), float(k), F32)
    thr = jnp.full((1, s.shape[1]), jnp.inf, F32)
    for _ in range(k):
        m = jnp.max(cur, axis=0, keepdims=True)
        eq = cur == m
        thr = jnp.where(left > 0.0, m, thr)
        left = left - jnp.sum(eq.astype(F32), axis=0, keepdims=True)
        cur = jnp.where(eq, -jnp.inf, cur)
    return thr


def _pruned_candidates(t0, t1):
    k = t0.shape[0]
    return jnp.concatenate([t0[a:a + 1, :] + t1[0:k // (a + 1), :] for a in range(k)], axis=0)


def _peer_kernel(x_ref, g_ref, wqt_ref, keys_ref, u_ref, vt_ref, o_ref,
                 hb_s, qt_s, sc_s, r1_s, e1_s, cnt_s, a0_s, rows_s, hut_s, coef_s, acc_s, *, tm, ec):
    c = pl.program_id(1)
    nc = pl.num_programs(1)
    nblk = ec // PEER_KEYS

    @pl.when(c == 0)
    def _():
        hb = _rmsnorm(x_ref[...], g_ref[...]).astype(BF)
        hb_s[...] = hb
        qt_s[...] = lax.dot_general(wqt_ref[...], hb, (((1,), (1,)), ((), ())),
                                    preferred_element_type=F32).astype(BF)
        for h in range(PEER_HEADS):
            for p in range(2):
                r0 = (h * 2 + p) * PEER_KEYS
                sc_s[p] = jnp.dot(keys_ref[h * 2 + p], qt_s[r0:r0 + PEER_KEYS, :], preferred_element_type=F32)
            rs = slice(h * PEER_KEYS, (h + 1) * PEER_KEYS)
            for cb in range(tm // LANES):
                ls = slice(cb * LANES, (cb + 1) * LANES)
                s0 = sc_s[0, :, ls]
                s1 = sc_s[1, :, ls]
                t0 = _top_values(s0, PEER_TOPK)
                t1, rank1 = _top_values(s1, PEER_TOPK, with_rank=True)
                cand = _pruned_candidates(t0, t1)
                thr = _kth_largest(cand, PEER_TOPK)
                m0 = t0[0:1, :]
                m1 = t1[0:1, :]
                zsum = jnp.sum(jnp.where(cand >= thr, jnp.exp(cand - (m0 + m1)), 0.0), axis=0, keepdims=True)
                cnt = jnp.zeros_like(s0)
                for b in range(PEER_TOPK):
                    keep = (t0 + t1[b:b + 1, :]) >= thr
                    sigma = jnp.min(jnp.where(keep, t0, jnp.inf), axis=0, keepdims=True)
                    cnt = cnt + (s0 >= sigma).astype(F32)
                r1_s[rs, ls] = rank1.astype(r1_s.dtype)
                e1_s[rs, ls] = (jnp.exp(s1 - m1) / zsum).astype(e1_s.dtype)
                cnt_s[rs, ls] = cnt
                a0_s[rs, ls] = jnp.exp(s0 - m0)
        acc_s[...] = jnp.zeros_like(acc_s)

    for h in range(PEER_HEADS):
        grp = pl.multiple_of(h * PEER_KEYS + c * nblk, SUBLANES)
        rows_s[0, h * nblk:(h + 1) * nblk, :] = cnt_s[pl.ds(grp, nblk), :]
        rows_s[1, h * nblk:(h + 1) * nblk, :] = a0_s[pl.ds(grp, nblk), :]

    sqrt_half = math.sqrt(0.5)
    gdt = r1_s.dtype
    for pc in range(ec // PEER_PIECE):
        es = slice(pc * PEER_PIECE, (pc + 1) * PEER_PIECE)
        hut_s[es, :] = lax.dot_general(u_ref[es, :], hb_s[...], (((1,), (1,)), ((), ())),
                                       preferred_element_type=F32)
        for il in range(pc * PEER_PIECE // PEER_KEYS, (pc + 1) * PEER_PIECE // PEER_KEYS):
            for cb in range(tm // LANES):
                ls = slice(cb * LANES, (cb + 1) * LANES)
                gate = jnp.zeros((PEER_KEYS, LANES), gdt)
                for h in range(PEER_HEADS):
                    rs = slice(h * PEER_KEYS, (h + 1) * PEER_KEYS)
                    cnt_row = rows_s[0, h * nblk + il:h * nblk + il + 1, ls].astype(gdt)
                    a0_row = rows_s[1, h * nblk + il:h * nblk + il + 1, ls].astype(gdt)
                    sel = r1_s[rs, ls] < cnt_row
                    gate = gate + jnp.where(sel, e1_s[rs, ls], jnp.zeros((), gdt)) * a0_row
                hu = hut_s[il * PEER_KEYS:(il + 1) * PEER_KEYS, ls]
                act = 0.5 * hu * (1.0 + lax.erf(hu * sqrt_half))
                coef_s[il * PEER_KEYS:(il + 1) * PEER_KEYS, ls] = (gate * act.astype(gdt)).astype(coef_s.dtype)
        acc_s[...] += jnp.dot(vt_ref[:, es], coef_s[es, :], preferred_element_type=F32)

    @pl.when(c == nc - 1)
    def _():
        o_ref[...] = x_ref[...] + acc_s[...].T


def _peer(x, lw, *, tm, ec):
    n = x.shape[0]
    assert n % tm == 0 and tm % MXU_TILE == 0 and ec % (PEER_KEYS * SUBLANES) == 0 and ec % PEER_PIECE == 0
    tok = pl.BlockSpec((tm, D_MODEL), lambda i, c: (i, 0))
    nq = PEER_HEADS * 2 * PEER_KEYS
    return pl.pallas_call(
        functools.partial(_peer_kernel, tm=tm, ec=ec),
        grid=(n // tm, N_EXPERTS // ec),
        in_specs=[tok,
                  pl.BlockSpec((1, D_MODEL), lambda i, c: (0, 0)),
                  pl.BlockSpec((nq, D_MODEL), lambda i, c: (0, 0), pipeline_mode=pl.Buffered(1)),
                  pl.BlockSpec((PEER_HEADS * 2, PEER_KEYS, PEER_KEYS), lambda i, c: (0, 0, 0),
                               pipeline_mode=pl.Buffered(1)),
                  pl.BlockSpec((ec, D_MODEL), lambda i, c: (c, 0)),
                  pl.BlockSpec((D_MODEL, ec), lambda i, c: (0, c))],
        out_specs=tok,
        out_shape=jax.ShapeDtypeStruct((n, D_MODEL), F32),
        scratch_shapes=[
            pltpu.VMEM((tm, D_MODEL), BF),
            pltpu.VMEM((nq, tm), BF),
            pltpu.VMEM((2, PEER_KEYS, tm), F32),
            pltpu.VMEM((PEER_HEADS * PEER_KEYS, tm), BF),
            pltpu.VMEM((PEER_HEADS * PEER_KEYS, tm), BF),
            pltpu.VMEM((PEER_HEADS * PEER_KEYS, tm), F32),
            pltpu.VMEM((PEER_HEADS * PEER_KEYS, tm), F32),
            pltpu.VMEM((2, PEER_HEADS * (ec // PEER_KEYS), tm), F32),
            pltpu.VMEM((ec, tm), F32),
            pltpu.VMEM((ec, tm), BF),
            pltpu.VMEM((D_MODEL, tm), F32),
        ],
        compiler_params=pltpu.CompilerParams(
            dimension_semantics=("arbitrary", "arbitrary"), vmem_limit_bytes=VMEM_LIMIT),
        name="peer",
    )(x, lw["norm_ffn"], lw["wq_t"], lw["keys"], lw["u"], lw["v_t"])


def _block_diag_pack(w):
    per = MXU_TILE // RNN_BLOCK
    nsup = w.shape[0] // per
    out = jnp.zeros((nsup, MXU_TILE, MXU_TILE), w.dtype)
    for s in range(nsup):
        for j in range(per):
            out = out.at[s, j * RNN_BLOCK:(j + 1) * RNN_BLOCK, j * RNN_BLOCK:(j + 1) * RNN_BLOCK].set(w[s * per + j])
    return out


def _layer_weights(l, norm_mix, norm_ffn, w_in, conv_w, conv_b, lru_wa, lru_ba, lru_wx, lru_bx, lru_lambda,
                   q_norm, k_norm, attn_sinks, pool_w, pool_scale, gate_b, w_branch, w_out,
                   peer_wq, peer_keys, peer_u, peer_v):
    w = w_in[l]
    o_q = 2 * D_RNN
    o_k = o_q + D_MODEL
    o_v = o_k + D_KV
    o_p = o_v + D_KV
    o_g = o_p + D_POOL
    ones_bd = jnp.asarray(np.kron(np.eye(MXU_TILE // HEAD_DIM), np.ones((HEAD_DIM, HEAD_DIM))), BF)
    return {
        "norm_mix": norm_mix[l][None, :],
        "norm_ffn": norm_ffn[l][None, :],
        "w_seq": jnp.concatenate([w[:, :o_q], w[:, o_p:o_g], w[:, o_k:o_p]], axis=1).astype(BF),
        "w_q": w[:, o_q:o_k].astype(BF),
        "w_g": w[:, o_g:].astype(BF),
        "conv_w": conv_w[l],
        "conv_b": conv_b[l][None, :],
        "wa": _block_diag_pack(lru_wa[l]).astype(BF),
        "wx": _block_diag_pack(lru_wx[l]).astype(BF),
        "ba": lru_ba[l][None, :],
        "bx": lru_bx[l][None, :],
        "lam": lru_lambda[l][None, :],
        "q_norm": jnp.tile(q_norm[l], N_HEADS)[None, :],
        "k_norm": jnp.tile(k_norm[l], N_KV)[None, :],
        "ones_bd": ones_bd,
        "sinks": attn_sinks[l],
        "pool_w": pool_w[l].astype(BF),
        "pool_scale": pool_scale[l][None, :],
        "gate_b": gate_b[l].reshape(1, N_BRANCH * D_MODEL),
        "w_branch": w_branch[l].astype(BF),
        "w_out": w_out[l].astype(BF),
        "wq_t": peer_wq[l].T.astype(BF),
        "keys": peer_keys[l].reshape(PEER_HEADS * 2, PEER_KEYS, PEER_KEYS).astype(BF),
        "u": peer_u[l].astype(BF),
        "v_t": peer_v[l].T.astype(BF),
    }


def _pick_tile(n, candidates):
    for t in candidates:
        if n % t == 0:
            return t
    raise ValueError(f"no tile for {n}")


def kernel(x_prompt, x_sample, state_rglru_h, state_rglru_conv, cache_meta_k, cache_meta_v, cache_win_k, cache_win_v, state_pool, meta_tokens, norm_mix, norm_ffn, w_in, conv_w, conv_b, lru_wa, lru_ba, lru_wx, lru_bx, lru_lambda, q_norm, k_norm, attn_sinks, pool_w, pool_scale, gate_b, w_branch, w_out, peer_wq, peer_keys, peer_u, peer_v):
    bp, seq, _ = x_prompt.shape
    bs, ts, _ = x_sample.shape
    depth = w_in.shape[0]
    dt = x_prompt.dtype
    s_pad = PROMPT_PAD + N_META + seq
    nchunks = s_pad // CHUNK
    cpt = 3 if nchunks % 3 == 0 else 1
    tt_p = cpt * CHUNK

    xp = jnp.concatenate([jnp.zeros((bp, PROMPT_PAD, D_MODEL), dt),
                          jnp.broadcast_to(meta_tokens.astype(dt)[None], (bp, N_META, D_MODEL)), x_prompt], axis=1)
    n_p = bp * s_pad
    n_s = bs * ts
    x = jnp.concatenate([xp.reshape(n_p, D_MODEL), x_sample.reshape(n_s, D_MODEL)], axis=0)
    tm = 2 * n_s
    ec = 2048

    zeros_ch = jnp.zeros((bp, SUBLANES, D_RNN), dt)
    zeros_h = jnp.zeros((bp, 1, D_RNN), dt)
    zeros_ph = jnp.zeros((bp, POOL_MAX, D_POOL), dt)

    outs = {k: [] for k in ("ph", "pc", "pmk", "pmv", "pwk", "pwv", "ppool", "sh", "sc", "sk", "sv", "spool")}
    for l in range(depth):
        lw = _layer_weights(l, norm_mix, norm_ffn, w_in, conv_w, conv_b, lru_wa, lru_ba, lru_wx, lru_bx,
                            lru_lambda, q_norm, k_norm, attn_sinks, pool_w, pool_scale, gate_b, w_branch,
                            w_out, peer_wq, peer_keys, peer_u, peer_v)
        ya, yc, k, v, cs, hl, pst = _seq_mixer(x, lw, zeros_ch, zeros_h, zeros_ph, nb=bp, nt=s_pad // tt_p,
                                               tt=tt_p, row0=0, pad=PROMPT_PAD, pool_hist=0)
        yb = _attn_prompt(x, k, v, lw, nb=bp, cpt=cpt)
        ys_prompt = (ya, yb, yc)
        k4 = k.reshape(bp, s_pad, N_KV, HEAD_DIM)
        v4 = v.reshape(bp, s_pad, N_KV, HEAD_DIM)
        outs["ph"].append(hl[:, 0])
        outs["pc"].append(cs[:, SUBLANES - (CONV_W - 1):])
        outs["pmk"].append(k4[:, PROMPT_PAD:PROMPT_PAD + N_META])
        outs["pmv"].append(v4[:, PROMPT_PAD:PROMPT_PAD + N_META])
        outs["pwk"].append(k4[:, -WINDOW:])
        outs["pwv"].append(v4[:, -WINDOW:])
        outs["ppool"].append(pst[:, 1:])
        ch8 = jnp.pad(state_rglru_conv[l], ((0, 0), (SUBLANES - (CONV_W - 1), 0), (0, 0)))
        ph16 = jnp.pad(state_pool[l], ((0, 0), (1, 0), (0, 0)))
        ya, yc, k, v, cs, hl, pst = _seq_mixer(x, lw, ch8, state_rglru_h[l][:, None, :], ph16, nb=bs, nt=1,
                                               tt=ts, row0=n_p, pad=0, pool_hist=POOL_MAX - 1)
        yb = _attn_sample(x, k, v, cache_meta_k[l].reshape(bs, N_META, D_KV),
                          cache_meta_v[l].reshape(bs, N_META, D_KV),
                          cache_win_k[l].reshape(bs, WINDOW, D_KV), cache_win_v[l].reshape(bs, WINDOW, D_KV),
                          lw, nb=bs, t_new=ts, row0=n_p)
        x = _merge(x, ys_prompt, (ya, yb, yc), lw, tm=tm)
        x = _peer(x, lw, tm=tm, ec=ec)
        outs["sh"].append(hl[:, 0])
        outs["sc"].append(cs[:, SUBLANES - (CONV_W - 1):])
        outs["sk"].append(k.reshape(bs, ts, N_KV, HEAD_DIM))
        outs["sv"].append(v.reshape(bs, ts, N_KV, HEAD_DIM))
        outs["spool"].append(pst[:, 1:])

    y_prompt = x[:n_p].reshape(bp, s_pad, D_MODEL)[:, PROMPT_PAD + N_META:]
    y_sample = x[n_p:].reshape(bs, ts, D_MODEL)
    st = lambda key: jnp.stack(outs[key])
    return (y_prompt, y_sample, st("ph"), st("pc"), st("pmk"), st("pmv"), st("pwk"), st("pwv"), st("ppool"),
            st("sh"), st("sc"), st("sk"), st("sv"), st("spool"))
```

```python
import functools
import math

import jax
import jax.numpy as jnp
import numpy as np
from jax import lax
from jax.experimental import pallas as pl
from jax.experimental.pallas import tpu as pltpu

D_MODEL = 1024
CHUNK = 64
N_META = 16
EPS = 1e-6
D_RNN = 1024
RNN_BLOCK = 64
CONV_W = 4
LRU_C = 8.0
HEAD_DIM = 64
N_HEADS = 16
N_KV = 4
GQA_GROUP = N_HEADS // N_KV
D_KV = N_KV * HEAD_DIM
WINDOW = 128
D_POOL = 1024
POOL_SIZES = (2, 4, 8, 16)
POOL_GROUP = 256
POOL_MAX = 16
N_BRANCH = 3
PEER_HEADS = 8
PEER_KEYS = 128
N_EXPERTS = PEER_KEYS * PEER_KEYS
PEER_TOPK = 16
PEER_PIECE = 512
NEG_INF = -1e30

LANES = 128
SUBLANES = 8
MXU_TILE = 256
VMEM_LIMIT = 56 * 1024 * 1024

PROMPT_PAD = CHUNK - N_META
SEQ_COLS = 2 * D_RNN + D_POOL + 2 * D_KV

BF = jnp.bfloat16
F32 = jnp.float32


def _rmsnorm(x, g):
    xf = x.astype(F32)
    return xf * lax.rsqrt(jnp.mean(xf * xf, axis=-1, keepdims=True) + EPS) * g


def _head_sumsq(x, ones_bd):
    sq = x * x
    hi = sq.astype(BF)
    lo = (sq - hi.astype(F32)).astype(BF)
    out = []
    for c in range(x.shape[1] // MXU_TILE):
        sl = slice(c * MXU_TILE, (c + 1) * MXU_TILE)
        out.append(jnp.dot(hi[:, sl], ones_bd, preferred_element_type=F32)
                   + jnp.dot(lo[:, sl], ones_bd, preferred_element_type=F32))
    return out[0] if len(out) == 1 else jnp.concatenate(out, axis=1)


def _head_rmsnorm(x, g, ones_bd):
    ms = _head_sumsq(x, ones_bd) * (1.0 / HEAD_DIM)
    return x * lax.rsqrt(ms + EPS) * g


def _shift_rows(x, d, fill, rows):
    return jnp.where(rows < d, fill, pltpu.roll(x, d, axis=0))


def _seq_kernel(x_ref, g_ref, w_ref, cw_ref, cb_ref, wa_ref, wx_ref, ba_ref, bx_ref, lam_ref,
                kn_ref, ones_ref, pw_ref, ps_ref, ch_ref, h0_ref, ph_ref,
                ya_ref, yc_ref, k_ref, v_ref, cs_ref, hl_ref, pst_ref,
                exr, exp_, hc, *, tt, pad, pool_hist):
    t = pl.program_id(1)

    @pl.when(t == 0)
    def _():
        exr[0:SUBLANES, :] = ch_ref[0]
        exp_[0:POOL_MAX, :] = ph_ref[0]
        hc[...] = jnp.broadcast_to(h0_ref[0], hc.shape)

    hb = _rmsnorm(x_ref[...], g_ref[...]).astype(BF)
    z = jnp.dot(hb, w_ref[...], preferred_element_type=F32)
    xr = z[:, 0:D_RNN]
    gr = z[:, D_RNN:2 * D_RNN]
    xp = z[:, 2 * D_RNN:2 * D_RNN + D_POOL]
    kk = z[:, 2 * D_RNN + D_POOL:2 * D_RNN + D_POOL + D_KV]
    vv = z[:, 2 * D_RNN + D_POOL + D_KV:]

    rows = lax.broadcasted_iota(jnp.int32, (tt, 1), 0)
    pos = rows + (t * tt - pad)

    exr[SUBLANES:SUBLANES + tt, :] = xr
    xc = cb_ref[...] + cw_ref[3:4, :] * xr
    for j in range(CONV_W - 1):
        xc = xc + cw_ref[j:j + 1, :] * exr[SUBLANES - (CONV_W - 1) + j:SUBLANES - (CONV_W - 1) + j + tt, :]
    exr[0:SUBLANES, :] = exr[tt:tt + SUBLANES, :]
    cs_ref[0] = exr[0:SUBLANES, :]

    ra, ia = [], []
    for c in range(D_RNN // MXU_TILE):
        xcb = xc[:, c * MXU_TILE:(c + 1) * MXU_TILE].astype(BF)
        ra.append(jnp.dot(xcb, wa_ref[c], preferred_element_type=F32))
        ia.append(jnp.dot(xcb, wx_ref[c], preferred_element_type=F32))
    r = jax.nn.sigmoid(jnp.concatenate(ra, axis=1) + ba_ref[...])
    i = jax.nn.sigmoid(jnp.concatenate(ia, axis=1) + bx_ref[...])
    nl = -lam_ref[...]
    softplus = jnp.maximum(nl, 0.0) + jnp.log1p(jnp.exp(-jnp.abs(nl)))
    log_a = (-LRU_C * softplus) * r
    a = jnp.exp(log_a)
    u = jnp.sqrt(1.0 - jnp.exp(2.0 * log_a)) * (i * xc)
    if pad:
        u = jnp.where(pos >= 0, u, 0.0)

    d = 1
    while d < tt:
        a_s = _shift_rows(a, d, 1.0, rows)
        u_s = _shift_rows(u, d, 0.0, rows)
        u = a * u_s + u
        a = a * a_s
        d *= 2
    h = a * hc[0:1, :] + u
    hc[...] = jnp.broadcast_to(h[tt - 1:tt, :], hc.shape)
    hl_ref[0] = h[tt - 1:tt, :]
    ya_ref[...] = (h * jax.nn.gelu(gr, approximate=True)).astype(ya_ref.dtype)

    k_ref[...] = _head_rmsnorm(kk, kn_ref[...], ones_ref[...])
    v_ref[...] = vv

    exp_[POOL_MAX:POOL_MAX + tt, :] = xp
    ext = exp_[...]
    yc = []
    for gi, size in enumerate(POOL_SIZES):
        sl = slice(gi * POOL_GROUP, (gi + 1) * POOL_GROUP)
        s = ext[:, sl]
        step = 1
        while step < size:
            s = s + pltpu.roll(s, step, axis=0)
            step *= 2
        cnt = jnp.clip(pos + (pool_hist + 1), 1, size).astype(F32)
        pooled = s[POOL_MAX:, :] / cnt - xp[:, sl]
        yc.append(jnp.dot(pooled.astype(BF), pw_ref[gi], preferred_element_type=F32))
    yc_ref[...] = (jnp.concatenate(yc, axis=1) * ps_ref[...]).astype(yc_ref.dtype)
    exp_[0:POOL_MAX, :] = exp_[tt:tt + POOL_MAX, :]
    pst_ref[0] = exp_[0:POOL_MAX, :]


def _seq_mixer(x, lw, conv_hist8, h0, pool_hist16, *, nb, nt, tt, row0, pad, pool_hist):
    n = nb * nt * tt
    assert row0 % tt == 0
    xrow = lambda b, t: (row0 // tt + b * nt + t, 0)
    row = lambda b, t: (b * nt + t, 0)
    full2 = lambda b, t: (0, 0)
    full3 = lambda b, t: (0, 0, 0)
    per_b = lambda b, t: (b, 0, 0)
    kern = functools.partial(_seq_kernel, tt=tt, pad=pad, pool_hist=pool_hist)
    return pl.pallas_call(
        kern,
        grid=(nb, nt),
        in_specs=[
            pl.BlockSpec((tt, D_MODEL), xrow),
            pl.BlockSpec((1, D_MODEL), full2),
            pl.BlockSpec((D_MODEL, SEQ_COLS), full2),
            pl.BlockSpec((CONV_W, D_RNN), full2),
            pl.BlockSpec((1, D_RNN), full2),
            pl.BlockSpec((D_RNN // MXU_TILE, MXU_TILE, MXU_TILE), full3),
            pl.BlockSpec((D_RNN // MXU_TILE, MXU_TILE, MXU_TILE), full3),
            pl.BlockSpec((1, D_RNN), full2),
            pl.BlockSpec((1, D_RNN), full2),
            pl.BlockSpec((1, D_RNN), full2),
            pl.BlockSpec((1, D_KV), full2),
            pl.BlockSpec((MXU_TILE, MXU_TILE), full2),
            pl.BlockSpec((len(POOL_SIZES), POOL_GROUP, POOL_GROUP), full3),
            pl.BlockSpec((1, D_POOL), full2),
            pl.BlockSpec((1, SUBLANES, D_RNN), per_b),
            pl.BlockSpec((1, 1, D_RNN), per_b),
            pl.BlockSpec((1, POOL_MAX, D_POOL), per_b),
        ],
        out_specs=[
            pl.BlockSpec((tt, D_RNN), row),
            pl.BlockSpec((tt, D_POOL), row),
            pl.BlockSpec((tt, D_KV), row),
            pl.BlockSpec((tt, D_KV), row),
            pl.BlockSpec((1, SUBLANES, D_RNN), per_b),
            pl.BlockSpec((1, 1, D_RNN), per_b),
            pl.BlockSpec((1, POOL_MAX, D_POOL), per_b),
        ],
        out_shape=[
            jax.ShapeDtypeStruct((n, D_RNN), BF),
            jax.ShapeDtypeStruct((n, D_POOL), BF),
            jax.ShapeDtypeStruct((n, D_KV), F32),
            jax.ShapeDtypeStruct((n, D_KV), F32),
            jax.ShapeDtypeStruct((nb, SUBLANES, D_RNN), F32),
            jax.ShapeDtypeStruct((nb, 1, D_RNN), F32),
            jax.ShapeDtypeStruct((nb, POOL_MAX, D_POOL), F32),
        ],
        scratch_shapes=[
            pltpu.VMEM((tt + SUBLANES, D_RNN), F32),
            pltpu.VMEM((tt + POOL_MAX, D_POOL), F32),
            pltpu.VMEM((SUBLANES, D_RNN), F32),
        ],
        compiler_params=pltpu.CompilerParams(
            dimension_semantics=("arbitrary", "arbitrary"), vmem_limit_bytes=VMEM_LIMIT),
        name="seq_mixer",
    )(x, lw["norm_mix"], lw["w_seq"], lw["conv_w"], lw["conv_b"], lw["wa"], lw["wx"], lw["ba"], lw["bx"],
      lw["lam"], lw["k_norm"], lw["ones_bd"], lw["pool_w"], lw["pool_scale"], conv_hist8, h0, pool_hist16)


def _attend(q4, kcat, vcat, sink_col, col_valid):
    lg = lax.dot_general(q4, kcat, (((1,), (1,)), ((), ())), preferred_element_type=F32) * (HEAD_DIM ** -0.5)
    if col_valid is not None:
        lg = jnp.where(col_valid, lg, NEG_INF)
    m = jnp.maximum(jnp.max(lg, axis=-1, keepdims=True), sink_col)
    e = jnp.exp(lg - m)
    p = e / (jnp.sum(e, axis=-1, keepdims=True) + jnp.exp(sink_col - m))
    return jnp.dot(p.astype(BF), vcat, preferred_element_type=F32)


def _project_q(x_ref, g_ref, wq_ref, qn_ref, ones_ref):
    hb = _rmsnorm(x_ref[...], g_ref[...]).astype(BF)
    q = jnp.dot(hb, wq_ref[...], preferred_element_type=F32)
    return _head_rmsnorm(q, qn_ref[...], ones_ref[...]).astype(BF)


def _sink_column(sink_ref, kh, rows_per_head):
    r = lax.broadcasted_iota(jnp.int32, (GQA_GROUP * rows_per_head, 1), 0)
    col = jnp.full((GQA_GROUP * rows_per_head, 1), sink_ref[kh * GQA_GROUP], F32)
    for gq in range(1, GQA_GROUP):
        col = jnp.where(r >= gq * rows_per_head, sink_ref[kh * GQA_GROUP + gq], col)
    return col


def _attn_prompt_kernel(sink_ref, x_ref, g_ref, wq_ref, qn_ref, ones_ref,
                        kc_ref, kh1_ref, kh2_ref, km_ref, vc_ref, vh1_ref, vh2_ref, vm_ref,
                        yb_ref, *, cpt):
    g = pl.program_id(1)
    q = _project_q(x_ref, g_ref, wq_ref, qn_ref, ones_ref)
    kwin = jnp.concatenate([kh1_ref[...], kh2_ref[...], kc_ref[...]], axis=0).astype(BF)
    vwin = jnp.concatenate([vh1_ref[...], vh2_ref[...], vc_ref[...]], axis=0).astype(BF)
    kmeta = km_ref[PROMPT_PAD:CHUNK, :].astype(BF)
    vmeta = vm_ref[PROMPT_PAD:CHUNK, :].astype(BF)
    n_keys = N_META + 3 * CHUNK
    col = lax.broadcasted_iota(jnp.int32, (1, n_keys), 1)
    for j in range(cpt):
        c = g * cpt + j
        first = N_META + CHUNK * jnp.clip(3 - c, 0, 3)
        first = jnp.where(c >= 1, first, n_keys)
        valid = (col < N_META) | (col >= first)
        for kh in range(N_KV):
            hs = slice(kh * HEAD_DIM, (kh + 1) * HEAD_DIM)
            kcat = jnp.concatenate([kmeta[:, hs], kwin[j * CHUNK:(j + 3) * CHUNK, hs]], axis=0)
            vcat = jnp.concatenate([vmeta[:, hs], vwin[j * CHUNK:(j + 3) * CHUNK, hs]], axis=0)
            q4 = jnp.concatenate(
                [q[j * CHUNK:(j + 1) * CHUNK, (kh * GQA_GROUP + gq) * HEAD_DIM:(kh * GQA_GROUP + gq + 1) * HEAD_DIM]
                 for gq in range(GQA_GROUP)], axis=0)
            o = _attend(q4, kcat, vcat, _sink_column(sink_ref, kh, CHUNK), valid)
            for gq in range(GQA_GROUP):
                hq = kh * GQA_GROUP + gq
                yb_ref[j * CHUNK:(j + 1) * CHUNK, hq * HEAD_DIM:(hq + 1) * HEAD_DIM] = (
                    o[gq * CHUNK:(gq + 1) * CHUNK, :].astype(yb_ref.dtype))


def _attn_prompt(x, k, v, lw, *, nb, cpt):
    n = k.shape[0]
    nchunks = n // (nb * CHUNK)
    nt = nchunks // cpt
    tt = cpt * CHUNK
    row = lambda b, g: (b * nt + g, 0)
    full2 = lambda b, g: (0, 0)
    halo1 = lambda b, g: (b * nchunks + jnp.maximum(g * cpt - 2, 0), 0)
    halo2 = lambda b, g: (b * nchunks + jnp.maximum(g * cpt - 1, 0), 0)
    meta = lambda b, g: (b * nchunks, 0)
    kv_specs = [pl.BlockSpec((tt, D_KV), row), pl.BlockSpec((CHUNK, D_KV), halo1),
                pl.BlockSpec((CHUNK, D_KV), halo2), pl.BlockSpec((CHUNK, D_KV), meta)]
    return pl.pallas_call(
        functools.partial(_attn_prompt_kernel, cpt=cpt),
        grid=(nb, nt),
        in_specs=[
            pl.BlockSpec(memory_space=pltpu.SMEM),
            pl.BlockSpec((tt, D_MODEL), row),
            pl.BlockSpec((1, D_MODEL), full2),
            pl.BlockSpec((D_MODEL, D_MODEL), full2),
            pl.BlockSpec((1, D_MODEL), full2),
            pl.BlockSpec((MXU_TILE, MXU_TILE), full2),
        ] + kv_specs + kv_specs,
        out_specs=pl.BlockSpec((tt, D_MODEL), row),
        out_shape=jax.ShapeDtypeStruct((n, D_MODEL), BF),
        compiler_params=pltpu.CompilerParams(
            dimension_semantics=("arbitrary", "arbitrary"), vmem_limit_bytes=VMEM_LIMIT),
        name="attn_prompt",
    )(lw["sinks"], x, lw["norm_mix"], lw["w_q"], lw["q_norm"], lw["ones_bd"], k, k, k, k, v, v, v, v)


def _attn_sample_kernel(sink_ref, x_ref, g_ref, wq_ref, qn_ref, ones_ref,
                        kn_ref, kw_ref, km_ref, vn_ref, vw_ref, vm_ref, yb_ref, *, t_new):
    q = _project_q(x_ref, g_ref, wq_ref, qn_ref, ones_ref)
    kall = jnp.concatenate([km_ref[0], kw_ref[0], kn_ref[...]], axis=0).astype(BF)
    vall = jnp.concatenate([vm_ref[0], vw_ref[0], vn_ref[...]], axis=0).astype(BF)
    for kh in range(N_KV):
        hs = slice(kh * HEAD_DIM, (kh + 1) * HEAD_DIM)
        q4 = jnp.concatenate(
            [q[:, (kh * GQA_GROUP + gq) * HEAD_DIM:(kh * GQA_GROUP + gq + 1) * HEAD_DIM]
             for gq in range(GQA_GROUP)], axis=0)
        o = _attend(q4, kall[:, hs], vall[:, hs], _sink_column(sink_ref, kh, t_new), None)
        for gq in range(GQA_GROUP):
            hq = kh * GQA_GROUP + gq
            yb_ref[:, hq * HEAD_DIM:(hq + 1) * HEAD_DIM] = o[gq * t_new:(gq + 1) * t_new, :].astype(yb_ref.dtype)


def _attn_sample(x, k, v, meta_k, meta_v, win_k, win_v, lw, *, nb, t_new, row0):
    n = nb * t_new
    assert row0 % t_new == 0
    xrow = lambda b: (row0 // t_new + b, 0)
    row = lambda b: (b, 0)
    full2 = lambda b: (0, 0)
    per_b = lambda b: (b, 0, 0)
    kv_specs = [pl.BlockSpec((t_new, D_KV), row), pl.BlockSpec((1, WINDOW, D_KV), per_b),
                pl.BlockSpec((1, N_META, D_KV), per_b)]
    return pl.pallas_call(
        functools.partial(_attn_sample_kernel, t_new=t_new),
        grid=(nb,),
        in_specs=[
            pl.BlockSpec(memory_space=pltpu.SMEM),
            pl.BlockSpec((t_new, D_MODEL), xrow),
            pl.BlockSpec((1, D_MODEL), full2),
            pl.BlockSpec((D_MODEL, D_MODEL), full2),
            pl.BlockSpec((1, D_MODEL), full2),
            pl.BlockSpec((MXU_TILE, MXU_TILE), full2),
        ] + kv_specs + kv_specs,
        out_specs=pl.BlockSpec((t_new, D_MODEL), row),
        out_shape=jax.ShapeDtypeStruct((n, D_MODEL), BF),
        compiler_params=pltpu.CompilerParams(
            dimension_semantics=("arbitrary",), vmem_limit_bytes=VMEM_LIMIT),
        name="attn_sample",
    )(lw["sinks"], x, lw["norm_mix"], lw["w_q"], lw["q_norm"], lw["ones_bd"], k, win_k, meta_k, v, win_v, meta_v)


def _merge_kernel(x_ref, *refs):
    y_refs, (g_ref, wg_ref, gb_ref, wb_ref, wo_ref, o_ref) = refs[:3 * N_BRANCH], refs[3 * N_BRANCH:]
    last = pl.program_id(0) == pl.num_programs(0) - 1
    x = x_ref[...]
    hb = _rmsnorm(x, g_ref[...]).astype(BF)
    merged = None
    for nbr in range(N_BRANCH):
        lo_ref, hi_ref, s_ref = y_refs[3 * nbr:3 * nbr + 3]
        y = jnp.concatenate([lo_ref[...], jnp.where(last, s_ref[...], hi_ref[...])], axis=0)
        gl = jnp.dot(hb, wg_ref[:, nbr * D_MODEL:(nbr + 1) * D_MODEL], preferred_element_type=F32)
        gate = jax.nn.sigmoid(gl + gb_ref[:, nbr * D_MODEL:(nbr + 1) * D_MODEL])
        term = gate * jnp.dot(y, wb_ref[nbr], preferred_element_type=F32)
        merged = term if merged is None else merged + term
    o_ref[...] = x + jnp.dot(merged.astype(BF), wo_ref[...], preferred_element_type=F32)


def _merge(x, ys_prompt, ys_sample, lw, *, tm):
    n = x.shape[0]
    half = tm // 2
    n_p = ys_prompt[0].shape[0]
    assert n % tm == 0 and n_p % tm == half and ys_sample[0].shape[0] == half
    last_half = n_p // half - 1
    row = lambda i: (i, 0)
    lo = lambda i: (jnp.minimum(2 * i, last_half), 0)
    hi = lambda i: (jnp.minimum(2 * i + 1, last_half), 0)
    full2 = lambda i: (0, 0)
    full3 = lambda i: (0, 0, 0)
    tok = pl.BlockSpec((tm, D_MODEL), row)
    y_specs, y_args = [], []
    for yp, ys in zip(ys_prompt, ys_sample):
        y_specs += [pl.BlockSpec((half, D_MODEL), lo), pl.BlockSpec((half, D_MODEL), hi),
                    pl.BlockSpec((half, D_MODEL), full2)]
        y_args += [yp, yp, ys]
    return pl.pallas_call(
        _merge_kernel,
        grid=(n // tm,),
        in_specs=[tok] + y_specs + [
                  pl.BlockSpec((1, D_MODEL), full2),
                  pl.BlockSpec((D_MODEL, N_BRANCH * D_MODEL), full2),
                  pl.BlockSpec((1, N_BRANCH * D_MODEL), full2),
                  pl.BlockSpec((N_BRANCH, D_MODEL, D_MODEL), full3),
                  pl.BlockSpec((D_MODEL, D_MODEL), full2)],
        out_specs=tok,
        out_shape=jax.ShapeDtypeStruct((n, D_MODEL), F32),
        compiler_params=pltpu.CompilerParams(
            dimension_semantics=("arbitrary",), vmem_limit_bytes=VMEM_LIMIT),
        name="merge",
    )(x, *y_args, lw["norm_mix"], lw["w_g"], lw["gate_b"], lw["w_branch"], lw["w_out"])


def _top_values(s, k, with_rank=False):
    vals = []
    cur = s
    rank = jnp.full(s.shape, float(k), F32)
    for i in range(k):
        m = jnp.max(cur, axis=0, keepdims=True)
        vals.append(m)
        eq = cur == m
        if with_rank:
            rank = jnp.where(eq, float(i), rank)
        cur = jnp.where(eq, -jnp.inf, cur)
    top = jnp.concatenate(vals, axis=0)
    return (top, rank) if with_rank else top


def _kth_largest(s, k):
    cur = s
    left = jnp.full((1, s.shape[1]), float(k), F32)
    thr = jnp.full((1, s.shape[1]), jnp.inf, F32)
    for _ in range(k):
        m = jnp.max(cur, axis=0, keepdims=True)
        eq = cur == m
        thr = jnp.where(left > 0.0, m, thr)
        left = left - jnp.sum(eq.astype(F32), axis=0, keepdims=True)
        cur = jnp.where(eq, -jnp.inf, cur)
    return thr


def _pruned_candidates(t0, t1):
    k = t0.shape[0]
    return jnp.concatenate([t0[a:a + 1, :] + t1[0:k // (a + 1), :] for a in range(k)], axis=0)


def _peer_kernel(x_ref, g_ref, wqt_ref, keys_ref, u_ref, vt_ref, o_ref,
                 hb_s, qt_s, sc_s, r1_s, e1_s, cnt_s, a0_s, rows_s, hut_s, coef_s, acc_s, *, tm, ec):
    c = pl.program_id(1)
    nc = pl.num_programs(1)
    nblk = ec // PEER_KEYS
    ncol = tm // LANES

    @pl.when(c == 0)
    def _():
        hb = _rmsnorm(x_ref[...], g_ref[...]).astype(BF)
        hb_s[...] = hb
        qt_s[...] = lax.dot_general(wqt_ref[...], hb, (((1,), (1,)), ((), ())),
                                    preferred_element_type=F32).astype(BF)
        for h in range(PEER_HEADS):
            for p in range(2):
                r0 = (h * 2 + p) * PEER_KEYS
                sc = jnp.dot(keys_ref[h * 2 + p], qt_s[r0:r0 + PEER_KEYS, :], preferred_element_type=F32)
                for cb in range(ncol):
                    sc_s[cb, p] = sc[:, cb * LANES:(cb + 1) * LANES]
            rs = slice(h * PEER_KEYS, (h + 1) * PEER_KEYS)
            for cb in range(ncol):
                s0 = sc_s[cb, 0]
                s1 = sc_s[cb, 1]
                t0 = _top_values(s0, PEER_TOPK)
                t1, rank1 = _top_values(s1, PEER_TOPK, with_rank=True)
                cand = _pruned_candidates(t0, t1)
                thr = _kth_largest(cand, PEER_TOPK)
                m0 = t0[0:1, :]
                m1 = t1[0:1, :]
                zsum = jnp.sum(jnp.where(cand >= thr, jnp.exp(cand - (m0 + m1)), 0.0), axis=0, keepdims=True)
                cnt = jnp.zeros_like(s0)
                for b in range(PEER_TOPK):
                    keep = (t0 + t1[b:b + 1, :]) >= thr
                    sigma = jnp.min(jnp.where(keep, t0, jnp.inf), axis=0, keepdims=True)
                    cnt = cnt + (s0 >= sigma).astype(F32)
                r1_s[cb, rs, :] = rank1.astype(r1_s.dtype)
                e1_s[cb, rs, :] = (jnp.exp(s1 - m1) / zsum).astype(e1_s.dtype)
                cnt_s[cb, rs, :] = cnt
                a0_s[cb, rs, :] = jnp.exp(s0 - m0)
        acc_s[...] = jnp.zeros_like(acc_s)

    for h in range(PEER_HEADS):
        grp = pl.multiple_of(h * PEER_KEYS + c * nblk, SUBLANES)
        for cb in range(ncol):
            rows_s[cb, 0, h * nblk:(h + 1) * nblk, :] = cnt_s[cb, pl.ds(grp, nblk), :]
            rows_s[cb, 1, h * nblk:(h + 1) * nblk, :] = a0_s[cb, pl.ds(grp, nblk), :]

    sqrt_half = math.sqrt(0.5)
    gdt = r1_s.dtype
    for pc in range(ec // PEER_PIECE):
        es = slice(pc * PEER_PIECE, (pc + 1) * PEER_PIECE)
        hut = lax.dot_general(u_ref[es, :], hb_s[...], (((1,), (1,)), ((), ())), preferred_element_type=F32)
        for cb in range(ncol):
            hut_s[cb, es, :] = hut[:, cb * LANES:(cb + 1) * LANES]
        for il in range(pc * PEER_PIECE // PEER_KEYS, (pc + 1) * PEER_PIECE // PEER_KEYS):
            bs = slice(il * PEER_KEYS, (il + 1) * PEER_KEYS)
            for cb in range(ncol):
                gate = jnp.zeros((PEER_KEYS, LANES), gdt)
                for h in range(PEER_HEADS):
                    rs = slice(h * PEER_KEYS, (h + 1) * PEER_KEYS)
                    cnt_row = rows_s[cb, 0, h * nblk + il:h * nblk + il + 1, :].astype(gdt)
                    a0_row = rows_s[cb, 1, h * nblk + il:h * nblk + il + 1, :].astype(gdt)
                    sel = r1_s[cb, rs, :] < cnt_row
                    gate = gate + jnp.where(sel, e1_s[cb, rs, :], jnp.zeros((), gdt)) * a0_row
                hu = hut_s[cb, bs, :]
                act = 0.5 * hu * (1.0 + lax.erf(hu * sqrt_half))
                coef_s[cb, bs, :] = (gate * act.astype(gdt)).astype(coef_s.dtype)
        coef = jnp.concatenate([coef_s[cb, es, :] for cb in range(ncol)], axis=1)
        acc_s[...] += jnp.dot(vt_ref[:, es], coef, preferred_element_type=F32)

    @pl.when(c == nc - 1)
    def _():
        o_ref[...] = x_ref[...] + acc_s[...].T


def _peer(x, lw, *, tm, ec):
    n = x.shape[0]
    assert n % tm == 0 and tm % MXU_TILE == 0 and ec % (PEER_KEYS * SUBLANES) == 0 and ec % PEER_PIECE == 0
    tok = pl.BlockSpec((tm, D_MODEL), lambda i, c: (i, 0))
    nq = PEER_HEADS * 2 * PEER_KEYS
    ncol = tm // LANES
    nkeys = PEER_HEADS * PEER_KEYS
    return pl.pallas_call(
        functools.partial(_peer_kernel, tm=tm, ec=ec),
        grid=(n // tm, N_EXPERTS // ec),
        in_specs=[tok,
                  pl.BlockSpec((1, D_MODEL), lambda i, c: (0, 0)),
                  pl.BlockSpec((nq, D_MODEL), lambda i, c: (0, 0), pipeline_mode=pl.Buffered(1)),
                  pl.BlockSpec((PEER_HEADS * 2, PEER_KEYS, PEER_KEYS), lambda i, c: (0, 0, 0),
                               pipeline_mode=pl.Buffered(1)),
                  pl.BlockSpec((ec, D_MODEL), lambda i, c: (c, 0)),
                  pl.BlockSpec((D_MODEL, ec), lambda i, c: (0, c))],
        out_specs=tok,
        out_shape=jax.ShapeDtypeStruct((n, D_MODEL), F32),
        scratch_shapes=[
            pltpu.VMEM((tm, D_MODEL), BF),
            pltpu.VMEM((nq, tm), BF),
            pltpu.VMEM((ncol, 2, PEER_KEYS, LANES), F32),
            pltpu.VMEM((ncol, nkeys, LANES), BF),
            pltpu.VMEM((ncol, nkeys, LANES), BF),
            pltpu.VMEM((ncol, nkeys, LANES), F32),
            pltpu.VMEM((ncol, nkeys, LANES), F32),
            pltpu.VMEM((ncol, 2, PEER_HEADS * (ec // PEER_KEYS), LANES), F32),
            pltpu.VMEM((ncol, ec, LANES), F32),
            pltpu.VMEM((ncol, ec, LANES), BF),
            pltpu.VMEM((D_MODEL, tm), F32),
        ],
        compiler_params=pltpu.CompilerParams(
            dimension_semantics=("arbitrary", "arbitrary"), vmem_limit_bytes=VMEM_LIMIT),
        name="peer",
    )(x, lw["norm_ffn"], lw["wq_t"], lw["keys"], lw["u"], lw["v_t"])


def _block_diag_pack(w):
    per = MXU_TILE // RNN_BLOCK
    nsup = w.shape[0] // per
    out = jnp.zeros((nsup, MXU_TILE, MXU_TILE), w.dtype)
    for s in range(nsup):
        for j in range(per):
            out = out.at[s, j * RNN_BLOCK:(j + 1) * RNN_BLOCK, j * RNN_BLOCK:(j + 1) * RNN_BLOCK].set(w[s * per + j])
    return out


def _layer_weights(l, norm_mix, norm_ffn, w_in, conv_w, conv_b, lru_wa, lru_ba, lru_wx, lru_bx, lru_lambda,
                   q_norm, k_norm, attn_sinks, pool_w, pool_scale, gate_b, w_branch, w_out,
                   peer_wq, peer_keys, peer_u, peer_v):
    w = w_in[l]
    o_q = 2 * D_RNN
    o_k = o_q + D_MODEL
    o_v = o_k + D_KV
    o_p = o_v + D_KV
    o_g = o_p + D_POOL
    ones_bd = jnp.asarray(np.kron(np.eye(MXU_TILE // HEAD_DIM), np.ones((HEAD_DIM, HEAD_DIM))), BF)
    return {
        "norm_mix": norm_mix[l][None, :],
        "norm_ffn": norm_ffn[l][None, :],
        "w_seq": jnp.concatenate([w[:, :o_q], w[:, o_p:o_g], w[:, o_k:o_p]], axis=1).astype(BF),
        "w_q": w[:, o_q:o_k].astype(BF),
        "w_g": w[:, o_g:].astype(BF),
        "conv_w": conv_w[l],
        "conv_b": conv_b[l][None, :],
        "wa": _block_diag_pack(lru_wa[l]).astype(BF),
        "wx": _block_diag_pack(lru_wx[l]).astype(BF),
        "ba": lru_ba[l][None, :],
        "bx": lru_bx[l][None, :],
        "lam": lru_lambda[l][None, :],
        "q_norm": jnp.tile(q_norm[l], N_HEADS)[None, :],
        "k_norm": jnp.tile(k_norm[l], N_KV)[None, :],
        "ones_bd": ones_bd,
        "sinks": attn_sinks[l],
        "pool_w": pool_w[l].astype(BF),
        "pool_scale": pool_scale[l][None, :],
        "gate_b": gate_b[l].reshape(1, N_BRANCH * D_MODEL),
        "w_branch": w_branch[l].astype(BF),
        "w_out": w_out[l].astype(BF),
        "wq_t": peer_wq[l].T.astype(BF),
        "keys": peer_keys[l].reshape(PEER_HEADS * 2, PEER_KEYS, PEER_KEYS).astype(BF),
        "u": peer_u[l].astype(BF),
        "v_t": peer_v[l].T.astype(BF),
    }


def kernel(x_prompt, x_sample, state_rglru_h, state_rglru_conv, cache_meta_k, cache_meta_v, cache_win_k, cache_win_v, state_pool, meta_tokens, norm_mix, norm_ffn, w_in, conv_w, conv_b, lru_wa, lru_ba, lru_wx, lru_bx, lru_lambda, q_norm, k_norm, attn_sinks, pool_w, pool_scale, gate_b, w_branch, w_out, peer_wq, peer_keys, peer_u, peer_v):
    bp, seq, _ = x_prompt.shape
    bs, ts, _ = x_sample.shape
    depth = w_in.shape[0]
    dt = x_prompt.dtype
    s_pad = PROMPT_PAD + N_META + seq
    nchunks = s_pad // CHUNK
    cpt = 3 if nchunks % 3 == 0 else 1
    tt_p = cpt * CHUNK

    xp = jnp.concatenate([jnp.zeros((bp, PROMPT_PAD, D_MODEL), dt),
                          jnp.broadcast_to(meta_tokens.astype(dt)[None], (bp, N_META, D_MODEL)), x_prompt], axis=1)
    n_p = bp * s_pad
    n_s = bs * ts
    x = jnp.concatenate([xp.reshape(n_p, D_MODEL), x_sample.reshape(n_s, D_MODEL)], axis=0)
    tm = 2 * n_s
    ec = 2048

    zeros_ch = jnp.zeros((bp, SUBLANES, D_RNN), dt)
    zeros_h = jnp.zeros((bp, 1, D_RNN), dt)
    zeros_ph = jnp.zeros((bp, POOL_MAX, D_POOL), dt)

    outs = {k: [] for k in ("ph", "pc", "pmk", "pmv", "pwk", "pwv", "ppool", "sh", "sc", "sk", "sv", "spool")}
    for l in range(depth):
        lw = _layer_weights(l, norm_mix, norm_ffn, w_in, conv_w, conv_b, lru_wa, lru_ba, lru_wx, lru_bx,
                            lru_lambda, q_norm, k_norm, attn_sinks, pool_w, pool_scale, gate_b, w_branch,
                            w_out, peer_wq, peer_keys, peer_u, peer_v)
        ya, yc, k, v, cs, hl, pst = _seq_mixer(x, lw, zeros_ch, zeros_h, zeros_ph, nb=bp, nt=s_pad // tt_p,
                                               tt=tt_p, row0=0, pad=PROMPT_PAD, pool_hist=0)
        yb = _attn_prompt(x, k, v, lw, nb=bp, cpt=cpt)
        ys_prompt = (ya, yb, yc)
        k4 = k.reshape(bp, s_pad, N_KV, HEAD_DIM)
        v4 = v.reshape(bp, s_pad, N_KV, HEAD_DIM)
        outs["ph"].append(hl[:, 0])
        outs["pc"].append(cs[:, SUBLANES - (CONV_W - 1):])
        outs["pmk"].append(k4[:, PROMPT_PAD:PROMPT_PAD + N_META])
        outs["pmv"].append(v4[:, PROMPT_PAD:PROMPT_PAD + N_META])
        outs["pwk"].append(k4[:, -WINDOW:])
        outs["pwv"].append(v4[:, -WINDOW:])
        outs["ppool"].append(pst[:, 1:])
        ch8 = jnp.pad(state_rglru_conv[l], ((0, 0), (SUBLANES - (CONV_W - 1), 0), (0, 0)))
        ph16 = jnp.pad(state_pool[l], ((0, 0), (1, 0), (0, 0)))
        ya, yc, k, v, cs, hl, pst = _seq_mixer(x, lw, ch8, state_rglru_h[l][:, None, :], ph16, nb=bs, nt=1,
                                               tt=ts, row0=n_p, pad=0, pool_hist=POOL_MAX - 1)
        yb = _attn_sample(x, k, v, cache_meta_k[l].reshape(bs, N_META, D_KV),
                          cache_meta_v[l].reshape(bs, N_META, D_KV),
                          cache_win_k[l].reshape(bs, WINDOW, D_KV), cache_win_v[l].reshape(bs, WINDOW, D_KV),
                          lw, nb=bs, t_new=ts, row0=n_p)
        x = _merge(x, ys_prompt, (ya, yb, yc), lw, tm=tm)
        x = _peer(x, lw, tm=tm, ec=ec)
        outs["sh"].append(hl[:, 0])
        outs["sc"].append(cs[:, SUBLANES - (CONV_W - 1):])
        outs["sk"].append(k.reshape(bs, ts, N_KV, HEAD_DIM))
        outs["sv"].append(v.reshape(bs, ts, N_KV, HEAD_DIM))
        outs["spool"].append(pst[:, 1:])

    y_prompt = x[:n_p].reshape(bp, s_pad, D_MODEL)[:, PROMPT_PAD + N_META:]
    y_sample = x[n_p:].reshape(bs, ts, D_MODEL)
    st = lambda key: jnp.stack(outs[key])
    return (y_prompt, y_sample, st("ph"), st("pc"), st("pmk"), st("pmv"), st("pwk"), st("pwv"), st("ppool"),
            st("sh"), st("sc"), st("sk"), st("sv"), st("spool"))
```

```python
import functools
import math

import jax
import jax.numpy as jnp
import numpy as np
from jax import lax
from jax.experimental import pallas as pl
from jax.experimental.pallas import tpu as pltpu

D_MODEL = 1024
CHUNK = 64
N_META = 16
EPS = 1e-6
D_RNN = 1024
RNN_BLOCK = 64
CONV_W = 4
LRU_C = 8.0
HEAD_DIM = 64
N_HEADS = 16
N_KV = 4
GQA_GROUP = N_HEADS // N_KV
D_KV = N_KV * HEAD_DIM
WINDOW = 128
D_POOL = 1024
POOL_SIZES = (2, 4, 8, 16)
POOL_GROUP = 256
POOL_MAX = 16
N_BRANCH = 3
PEER_HEADS = 8
PEER_KEYS = 128
N_EXPERTS = PEER_KEYS * PEER_KEYS
PEER_TOPK = 16
PEER_PIECE = 2048
NEG_INF = -1e30

LANES = 128
SUBLANES = 8
MXU_TILE = 256
VMEM_LIMIT = 56 * 1024 * 1024

PROMPT_PAD = CHUNK - N_META
SEQ_COLS = 2 * D_RNN + D_POOL + 2 * D_KV

BF = jnp.bfloat16
F32 = jnp.float32


def _rmsnorm(x, g):
    xf = x.astype(F32)
    return xf * lax.rsqrt(jnp.mean(xf * xf, axis=-1, keepdims=True) + EPS) * g


def _head_sumsq(x, ones_bd):
    sq = x * x
    hi = sq.astype(BF)
    lo = (sq - hi.astype(F32)).astype(BF)
    out = []
    for c in range(x.shape[1] // MXU_TILE):
        sl = slice(c * MXU_TILE, (c + 1) * MXU_TILE)
        out.append(jnp.dot(hi[:, sl], ones_bd, preferred_element_type=F32)
                   + jnp.dot(lo[:, sl], ones_bd, preferred_element_type=F32))
    return out[0] if len(out) == 1 else jnp.concatenate(out, axis=1)


def _head_rmsnorm(x, g, ones_bd):
    ms = _head_sumsq(x, ones_bd) * (1.0 / HEAD_DIM)
    return x * lax.rsqrt(ms + EPS) * g


def _shift_rows(x, d, fill, rows):
    return jnp.where(rows < d, fill, pltpu.roll(x, d, axis=0))


def _seq_kernel(x_ref, g_ref, w_ref, cw_ref, cb_ref, wa_ref, wx_ref, ba_ref, bx_ref, lam_ref,
                kn_ref, ones_ref, pw_ref, ps_ref, ch_ref, h0_ref, ph_ref,
                ya_ref, yc_ref, k_ref, v_ref, cs_ref, hl_ref, pst_ref,
                exr, exp_, hc, *, tt, pad, pool_hist):
    t = pl.program_id(1)

    @pl.when(t == 0)
    def _():
        exr[0:SUBLANES, :] = ch_ref[0]
        exp_[0:POOL_MAX, :] = ph_ref[0]
        hc[...] = jnp.broadcast_to(h0_ref[0], hc.shape)

    hb = _rmsnorm(x_ref[...], g_ref[...]).astype(BF)
    z = jnp.dot(hb, w_ref[...], preferred_element_type=F32)
    xr = z[:, 0:D_RNN]
    gr = z[:, D_RNN:2 * D_RNN]
    xp = z[:, 2 * D_RNN:2 * D_RNN + D_POOL]
    kk = z[:, 2 * D_RNN + D_POOL:2 * D_RNN + D_POOL + D_KV]
    vv = z[:, 2 * D_RNN + D_POOL + D_KV:]

    rows = lax.broadcasted_iota(jnp.int32, (tt, 1), 0)
    pos = rows + (t * tt - pad)

    exr[SUBLANES:SUBLANES + tt, :] = xr
    xc = cb_ref[...] + cw_ref[3:4, :] * xr
    for j in range(CONV_W - 1):
        xc = xc + cw_ref[j:j + 1, :] * exr[SUBLANES - (CONV_W - 1) + j:SUBLANES - (CONV_W - 1) + j + tt, :]
    exr[0:SUBLANES, :] = exr[tt:tt + SUBLANES, :]
    cs_ref[0] = exr[0:SUBLANES, :]

    ra, ia = [], []
    for c in range(D_RNN // MXU_TILE):
        xcb = xc[:, c * MXU_TILE:(c + 1) * MXU_TILE].astype(BF)
        ra.append(jnp.dot(xcb, wa_ref[c], preferred_element_type=F32))
        ia.append(jnp.dot(xcb, wx_ref[c], preferred_element_type=F32))
    r = jax.nn.sigmoid(jnp.concatenate(ra, axis=1) + ba_ref[...])
    i = jax.nn.sigmoid(jnp.concatenate(ia, axis=1) + bx_ref[...])
    nl = -lam_ref[...]
    softplus = jnp.maximum(nl, 0.0) + jnp.log1p(jnp.exp(-jnp.abs(nl)))
    log_a = (-LRU_C * softplus) * r
    a = jnp.exp(log_a)
    u = jnp.sqrt(1.0 - jnp.exp(2.0 * log_a)) * (i * xc)
    if pad:
        u = jnp.where(pos >= 0, u, 0.0)

    sub = rows % SUBLANES
    d = 1
    while d < SUBLANES:
        a_s = _shift_rows(a, d, 1.0, sub)
        u_s = _shift_rows(u, d, 0.0, sub)
        u = a * u_s + u
        a = a * a_s
        d *= 2
    carry = hc[0:1, :]
    groups = []
    for gi in range(tt // SUBLANES):
        hg = a[gi * SUBLANES:(gi + 1) * SUBLANES, :] * carry + u[gi * SUBLANES:(gi + 1) * SUBLANES, :]
        groups.append(hg)
        carry = hg[SUBLANES - 1:SUBLANES, :]
    h = jnp.concatenate(groups, axis=0)
    hc[...] = jnp.broadcast_to(h[tt - 1:tt, :], hc.shape)
    hl_ref[0] = h[tt - 1:tt, :]
    ya_ref[...] = (h * jax.nn.gelu(gr, approximate=True)).astype(ya_ref.dtype)

    k_ref[...] = _head_rmsnorm(kk, kn_ref[...], ones_ref[...])
    v_ref[...] = vv

    exp_[POOL_MAX:POOL_MAX + tt, :] = xp
    ext = exp_[...]
    yc = []
    for gi, size in enumerate(POOL_SIZES):
        sl = slice(gi * POOL_GROUP, (gi + 1) * POOL_GROUP)
        s = ext[:, sl]
        step = 1
        while step < size:
            s = s + pltpu.roll(s, step, axis=0)
            step *= 2
        cnt = jnp.clip(pos + (pool_hist + 1), 1, size).astype(F32)
        pooled = s[POOL_MAX:, :] / cnt - xp[:, sl]
        yc.append(jnp.dot(pooled.astype(BF), pw_ref[gi], preferred_element_type=F32))
    yc_ref[...] = (jnp.concatenate(yc, axis=1) * ps_ref[...]).astype(yc_ref.dtype)
    exp_[0:POOL_MAX, :] = exp_[tt:tt + POOL_MAX, :]
    pst_ref[0] = exp_[0:POOL_MAX, :]


def _seq_mixer(x, lw, conv_hist8, h0, pool_hist16, *, nb, nt, tt, row0, pad, pool_hist):
    n = nb * nt * tt
    assert row0 % tt == 0
    xrow = lambda b, t: (row0 // tt + b * nt + t, 0)
    row = lambda b, t: (b * nt + t, 0)
    full2 = lambda b, t: (0, 0)
    full3 = lambda b, t: (0, 0, 0)
    per_b = lambda b, t: (b, 0, 0)
    kern = functools.partial(_seq_kernel, tt=tt, pad=pad, pool_hist=pool_hist)
    return pl.pallas_call(
        kern,
        grid=(nb, nt),
        in_specs=[
            pl.BlockSpec((tt, D_MODEL), xrow),
            pl.BlockSpec((1, D_MODEL), full2),
            pl.BlockSpec((D_MODEL, SEQ_COLS), full2),
            pl.BlockSpec((CONV_W, D_RNN), full2),
            pl.BlockSpec((1, D_RNN), full2),
            pl.BlockSpec((D_RNN // MXU_TILE, MXU_TILE, MXU_TILE), full3),
            pl.BlockSpec((D_RNN // MXU_TILE, MXU_TILE, MXU_TILE), full3),
            pl.BlockSpec((1, D_RNN), full2),
            pl.BlockSpec((1, D_RNN), full2),
            pl.BlockSpec((1, D_RNN), full2),
            pl.BlockSpec((1, D_KV), full2),
            pl.BlockSpec((MXU_TILE, MXU_TILE), full2),
            pl.BlockSpec((len(POOL_SIZES), POOL_GROUP, POOL_GROUP), full3),
            pl.BlockSpec((1, D_POOL), full2),
            pl.BlockSpec((1, SUBLANES, D_RNN), per_b),
            pl.BlockSpec((1, 1, D_RNN), per_b),
            pl.BlockSpec((1, POOL_MAX, D_POOL), per_b),
        ],
        out_specs=[
            pl.BlockSpec((tt, D_RNN), row),
            pl.BlockSpec((tt, D_POOL), row),
            pl.BlockSpec((tt, D_KV), row),
            pl.BlockSpec((tt, D_KV), row),
            pl.BlockSpec((1, SUBLANES, D_RNN), per_b),
            pl.BlockSpec((1, 1, D_RNN), per_b),
            pl.BlockSpec((1, POOL_MAX, D_POOL), per_b),
        ],
        out_shape=[
            jax.ShapeDtypeStruct((n, D_RNN), BF),
            jax.ShapeDtypeStruct((n, D_POOL), BF),
            jax.ShapeDtypeStruct((n, D_KV), F32),
            jax.ShapeDtypeStruct((n, D_KV), F32),
            jax.ShapeDtypeStruct((nb, SUBLANES, D_RNN), F32),
            jax.ShapeDtypeStruct((nb, 1, D_RNN), F32),
            jax.ShapeDtypeStruct((nb, POOL_MAX, D_POOL), F32),
        ],
        scratch_shapes=[
            pltpu.VMEM((tt + SUBLANES, D_RNN), F32),
            pltpu.VMEM((tt + POOL_MAX, D_POOL), F32),
            pltpu.VMEM((SUBLANES, D_RNN), F32),
        ],
        compiler_params=pltpu.CompilerParams(
            dimension_semantics=("arbitrary", "arbitrary"), vmem_limit_bytes=VMEM_LIMIT),
        name="seq_mixer",
    )(x, lw["norm_mix"], lw["w_seq"], lw["conv_w"], lw["conv_b"], lw["wa"], lw["wx"], lw["ba"], lw["bx"],
      lw["lam"], lw["k_norm"], lw["ones_bd"], lw["pool_w"], lw["pool_scale"], conv_hist8, h0, pool_hist16)


def _attend(q4, kcat, vcat, sink_col, col_valid):
    lg = lax.dot_general(q4, kcat, (((1,), (1,)), ((), ())), preferred_element_type=F32) * (HEAD_DIM ** -0.5)
    if col_valid is not None:
        lg = jnp.where(col_valid, lg, NEG_INF)
    m = jnp.maximum(jnp.max(lg, axis=-1, keepdims=True), sink_col)
    e = jnp.exp(lg - m)
    p = e / (jnp.sum(e, axis=-1, keepdims=True) + jnp.exp(sink_col - m))
    return jnp.dot(p.astype(BF), vcat, preferred_element_type=F32)


def _project_q(x_ref, g_ref, wq_ref, qn_ref, ones_ref):
    hb = _rmsnorm(x_ref[...], g_ref[...]).astype(BF)
    q = jnp.dot(hb, wq_ref[...], preferred_element_type=F32)
    return _head_rmsnorm(q, qn_ref[...], ones_ref[...]).astype(BF)


def _sink_column(sink_ref, kh, rows_per_head):
    r = lax.broadcasted_iota(jnp.int32, (GQA_GROUP * rows_per_head, 1), 0)
    col = jnp.full((GQA_GROUP * rows_per_head, 1), sink_ref[kh * GQA_GROUP], F32)
    for gq in range(1, GQA_GROUP):
        col = jnp.where(r >= gq * rows_per_head, sink_ref[kh * GQA_GROUP + gq], col)
    return col


def _attn_prompt_kernel(sink_ref, x_ref, g_ref, wq_ref, qn_ref, ones_ref,
                        kc_ref, kh1_ref, kh2_ref, km_ref, vc_ref, vh1_ref, vh2_ref, vm_ref,
                        yb_ref, *, cpt):
    g = pl.program_id(1)
    q = _project_q(x_ref, g_ref, wq_ref, qn_ref, ones_ref)
    kwin = jnp.concatenate([kh1_ref[...], kh2_ref[...], kc_ref[...]], axis=0).astype(BF)
    vwin = jnp.concatenate([vh1_ref[...], vh2_ref[...], vc_ref[...]], axis=0).astype(BF)
    kmeta = km_ref[PROMPT_PAD:CHUNK, :].astype(BF)
    vmeta = vm_ref[PROMPT_PAD:CHUNK, :].astype(BF)
    n_keys = N_META + 3 * CHUNK
    col = lax.broadcasted_iota(jnp.int32, (1, n_keys), 1)
    for j in range(cpt):
        c = g * cpt + j
        first = N_META + CHUNK * jnp.clip(3 - c, 0, 3)
        first = jnp.where(c >= 1, first, n_keys)
        valid = (col < N_META) | (col >= first)
        for kh in range(N_KV):
            hs = slice(kh * HEAD_DIM, (kh + 1) * HEAD_DIM)
            kcat = jnp.concatenate([kmeta[:, hs], kwin[j * CHUNK:(j + 3) * CHUNK, hs]], axis=0)
            vcat = jnp.concatenate([vmeta[:, hs], vwin[j * CHUNK:(j + 3) * CHUNK, hs]], axis=0)
            q4 = jnp.concatenate(
                [q[j * CHUNK:(j + 1) * CHUNK, (kh * GQA_GROUP + gq) * HEAD_DIM:(kh * GQA_GROUP + gq + 1) * HEAD_DIM]
                 for gq in range(GQA_GROUP)], axis=0)
            o = _attend(q4, kcat, vcat, _sink_column(sink_ref, kh, CHUNK), valid)
            for gq in range(GQA_GROUP):
                hq = kh * GQA_GROUP + gq
                yb_ref[j * CHUNK:(j + 1) * CHUNK, hq * HEAD_DIM:(hq + 1) * HEAD_DIM] = (
                    o[gq * CHUNK:(gq + 1) * CHUNK, :].astype(yb_ref.dtype))


def _attn_prompt(x, k, v, lw, *, nb, cpt):
    n = k.shape[0]
    nchunks = n // (nb * CHUNK)
    nt = nchunks // cpt
    tt = cpt * CHUNK
    row = lambda b, g: (b * nt + g, 0)
    full2 = lambda b, g: (0, 0)
    halo1 = lambda b, g: (b * nchunks + jnp.maximum(g * cpt - 2, 0), 0)
    halo2 = lambda b, g: (b * nchunks + jnp.maximum(g * cpt - 1, 0), 0)
    meta = lambda b, g: (b * nchunks, 0)
    kv_specs = [pl.BlockSpec((tt, D_KV), row), pl.BlockSpec((CHUNK, D_KV), halo1),
                pl.BlockSpec((CHUNK, D_KV), halo2), pl.BlockSpec((CHUNK, D_KV), meta)]
    return pl.pallas_call(
        functools.partial(_attn_prompt_kernel, cpt=cpt),
        grid=(nb, nt),
        in_specs=[
            pl.BlockSpec(memory_space=pltpu.SMEM),
            pl.BlockSpec((tt, D_MODEL), row),
            pl.BlockSpec((1, D_MODEL), full2),
            pl.BlockSpec((D_MODEL, D_MODEL), full2),
            pl.BlockSpec((1, D_MODEL), full2),
            pl.BlockSpec((MXU_TILE, MXU_TILE), full2),
        ] + kv_specs + kv_specs,
        out_specs=pl.BlockSpec((tt, D_MODEL), row),
        out_shape=jax.ShapeDtypeStruct((n, D_MODEL), BF),
        compiler_params=pltpu.CompilerParams(
            dimension_semantics=("arbitrary", "arbitrary"), vmem_limit_bytes=VMEM_LIMIT),
        name="attn_prompt",
    )(lw["sinks"], x, lw["norm_mix"], lw["w_q"], lw["q_norm"], lw["ones_bd"], k, k, k, k, v, v, v, v)


def _attn_sample_kernel(sink_ref, x_ref, g_ref, wq_ref, qn_ref, ones_ref,
                        kn_ref, kw_ref, km_ref, vn_ref, vw_ref, vm_ref, yb_ref, *, t_new):
    q = _project_q(x_ref, g_ref, wq_ref, qn_ref, ones_ref)
    kall = jnp.concatenate([km_ref[0], kw_ref[0], kn_ref[...]], axis=0).astype(BF)
    vall = jnp.concatenate([vm_ref[0], vw_ref[0], vn_ref[...]], axis=0).astype(BF)
    for kh in range(N_KV):
        hs = slice(kh * HEAD_DIM, (kh + 1) * HEAD_DIM)
        q4 = jnp.concatenate(
            [q[:, (kh * GQA_GROUP + gq) * HEAD_DIM:(kh * GQA_GROUP + gq + 1) * HEAD_DIM]
             for gq in range(GQA_GROUP)], axis=0)
        o = _attend(q4, kall[:, hs], vall[:, hs], _sink_column(sink_ref, kh, t_new), None)
        for gq in range(GQA_GROUP):
            hq = kh * GQA_GROUP + gq
            yb_ref[:, hq * HEAD_DIM:(hq + 1) * HEAD_DIM] = o[gq * t_new:(gq + 1) * t_new, :].astype(yb_ref.dtype)


def _attn_sample(x, k, v, meta_k, meta_v, win_k, win_v, lw, *, nb, t_new, row0):
    n = nb * t_new
    assert row0 % t_new == 0
    xrow = lambda b: (row0 // t_new + b, 0)
    row = lambda b: (b, 0)
    full2 = lambda b: (0, 0)
    per_b = lambda b: (b, 0, 0)
    kv_specs = [pl.BlockSpec((t_new, D_KV), row), pl.BlockSpec((1, WINDOW, D_KV), per_b),
                pl.BlockSpec((1, N_META, D_KV), per_b)]
    return pl.pallas_call(
        functools.partial(_attn_sample_kernel, t_new=t_new),
        grid=(nb,),
        in_specs=[
            pl.BlockSpec(memory_space=pltpu.SMEM),
            pl.BlockSpec((t_new, D_MODEL), xrow),
            pl.BlockSpec((1, D_MODEL), full2),
            pl.BlockSpec((D_MODEL, D_MODEL), full2),
            pl.BlockSpec((1, D_MODEL), full2),
            pl.BlockSpec((MXU_TILE, MXU_TILE), full2),
        ] + kv_specs + kv_specs,
        out_specs=pl.BlockSpec((t_new, D_MODEL), row),
        out_shape=jax.ShapeDtypeStruct((n, D_MODEL), BF),
        compiler_params=pltpu.CompilerParams(
            dimension_semantics=("arbitrary",), vmem_limit_bytes=VMEM_LIMIT),
        name="attn_sample",
    )(lw["sinks"], x, lw["norm_mix"], lw["w_q"], lw["q_norm"], lw["ones_bd"], k, win_k, meta_k, v, win_v, meta_v)


def _merge_kernel(x_ref, *refs):
    y_refs, (g_ref, wg_ref, gb_ref, wb_ref, wo_ref, o_ref) = refs[:3 * N_BRANCH], refs[3 * N_BRANCH:]
    last = pl.program_id(0) == pl.num_programs(0) - 1
    x = x_ref[...]
    hb = _rmsnorm(x, g_ref[...]).astype(BF)
    merged = None
    for nbr in range(N_BRANCH):
        lo_ref, hi_ref, s_ref = y_refs[3 * nbr:3 * nbr + 3]
        y = jnp.concatenate([lo_ref[...], jnp.where(last, s_ref[...], hi_ref[...])], axis=0)
        gl = jnp.dot(hb, wg_ref[:, nbr * D_MODEL:(nbr + 1) * D_MODEL], preferred_element_type=F32)
        gate = jax.nn.sigmoid(gl + gb_ref[:, nbr * D_MODEL:(nbr + 1) * D_MODEL])
        term = gate * jnp.dot(y, wb_ref[nbr], preferred_element_type=F32)
        merged = term if merged is None else merged + term
    o_ref[...] = x + jnp.dot(merged.astype(BF), wo_ref[...], preferred_element_type=F32)


def _merge(x, ys_prompt, ys_sample, lw, *, tm):
    n = x.shape[0]
    half = tm // 2
    n_p = ys_prompt[0].shape[0]
    assert n % tm == 0 and n_p % tm == half and ys_sample[0].shape[0] == half
    last_half = n_p // half - 1
    row = lambda i: (i, 0)
    lo = lambda i: (jnp.minimum(2 * i, last_half), 0)
    hi = lambda i: (jnp.minimum(2 * i + 1, last_half), 0)
    full2 = lambda i: (0, 0)
    full3 = lambda i: (0, 0, 0)
    tok = pl.BlockSpec((tm, D_MODEL), row)
    y_specs, y_args = [], []
    for yp, ys in zip(ys_prompt, ys_sample):
        y_specs += [pl.BlockSpec((half, D_MODEL), lo), pl.BlockSpec((half, D_MODEL), hi),
                    pl.BlockSpec((half, D_MODEL), full2)]
        y_args += [yp, yp, ys]
    return pl.pallas_call(
        _merge_kernel,
        grid=(n // tm,),
        in_specs=[tok] + y_specs + [
                  pl.BlockSpec((1, D_MODEL), full2),
                  pl.BlockSpec((D_MODEL, N_BRANCH * D_MODEL), full2),
                  pl.BlockSpec((1, N_BRANCH * D_MODEL), full2),
                  pl.BlockSpec((N_BRANCH, D_MODEL, D_MODEL), full3),
                  pl.BlockSpec((D_MODEL, D_MODEL), full2)],
        out_specs=tok,
        out_shape=jax.ShapeDtypeStruct((n, D_MODEL), F32),
        compiler_params=pltpu.CompilerParams(
            dimension_semantics=("arbitrary",), vmem_limit_bytes=VMEM_LIMIT),
        name="merge",
    )(x, *y_args, lw["norm_mix"], lw["w_g"], lw["gate_b"], lw["w_branch"], lw["w_out"])


def _top_values(s, k, with_rank=False):
    vals = []
    cur = s
    rank = jnp.full(s.shape, float(k), F32)
    for i in range(k):
        m = jnp.max(cur, axis=0, keepdims=True)
        vals.append(m)
        eq = cur == m
        if with_rank:
            rank = jnp.where(eq, float(i), rank)
        cur = jnp.where(eq, -jnp.inf, cur)
    top = jnp.concatenate(vals, axis=0)
    return (top, rank) if with_rank else top


def _kth_largest(s, k):
    cur = s
    left = jnp.full((1, s.shape[1]), float(k), F32)
    thr = jnp.full((1, s.shape[1]), jnp.inf, F32)
    for _ in range(k):
        m = jnp.max(cur, axis=0, keepdims=True)
        eq = cur == m
        thr = jnp.where(left > 0.0, m, thr)
        left = left - jnp.sum(eq.astype(F32), axis=0, keepdims=True)
        cur = jnp.where(eq, -jnp.inf, cur)
    return thr


def _pruned_candidates(t0, t1):
    k = t0.shape[0]
    return jnp.concatenate([t0[a:a + 1, :] + t1[0:k // (a + 1), :] for a in range(k)], axis=0)


def _peer_kernel(x_ref, g_ref, wqt_ref, keys_ref, u_ref, vt_ref, o_ref,
                 hb_s, qt_s, sc_s, r1_s, e1_s, cnt_s, a0_s, rows_s, hut_s, coef_s, acc_s, *, tm, ec):
    c = pl.program_id(1)
    nc = pl.num_programs(1)
    nblk = ec // PEER_KEYS
    ncol = tm // LANES

    @pl.when(c == 0)
    def _():
        hb = _rmsnorm(x_ref[...], g_ref[...]).astype(BF)
        hb_s[...] = hb
        qt_s[...] = lax.dot_general(wqt_ref[...], hb, (((1,), (1,)), ((), ())),
                                    preferred_element_type=F32).astype(BF)
        for h in range(PEER_HEADS):
            for p in range(2):
                r0 = (h * 2 + p) * PEER_KEYS
                sc_s[p] = jnp.dot(keys_ref[h * 2 + p], qt_s[r0:r0 + PEER_KEYS, :], preferred_element_type=F32)
            rs = slice(h * PEER_KEYS, (h + 1) * PEER_KEYS)
            for cb in range(ncol):
                ls = slice(cb * LANES, (cb + 1) * LANES)
                s0 = sc_s[0, :, ls]
                s1 = sc_s[1, :, ls]
                t0 = _top_values(s0, PEER_TOPK)
                t1, rank1 = _top_values(s1, PEER_TOPK, with_rank=True)
                cand = _pruned_candidates(t0, t1)
                thr = _kth_largest(cand, PEER_TOPK)
                m0 = t0[0:1, :]
                m1 = t1[0:1, :]
                zsum = jnp.sum(jnp.where(cand >= thr, jnp.exp(cand - (m0 + m1)), 0.0), axis=0, keepdims=True)
                cnt = jnp.zeros_like(s0)
                for b in range(PEER_TOPK):
                    keep = (t0 + t1[b:b + 1, :]) >= thr
                    sigma = jnp.min(jnp.where(keep, t0, jnp.inf), axis=0, keepdims=True)
                    cnt = cnt + (s0 >= sigma).astype(F32)
                r1_s[rs, ls] = rank1.astype(r1_s.dtype)
                e1_s[rs, ls] = (jnp.exp(s1 - m1) / zsum).astype(e1_s.dtype)
                cnt_s[rs, ls] = cnt
                a0_s[rs, ls] = jnp.exp(s0 - m0)
        acc_s[...] = jnp.zeros_like(acc_s)

    for h in range(PEER_HEADS):
        grp = pl.multiple_of(h * PEER_KEYS + c * nblk, SUBLANES)
        rows_s[0, h * nblk:(h + 1) * nblk, :] = cnt_s[pl.ds(grp, nblk), :]
        rows_s[1, h * nblk:(h + 1) * nblk, :] = a0_s[pl.ds(grp, nblk), :]

    sqrt_half = math.sqrt(0.5)
    gdt = r1_s.dtype
    for pc in range(ec // PEER_PIECE):
        es = slice(pc * PEER_PIECE, (pc + 1) * PEER_PIECE)
        hut_s[es, :] = lax.dot_general(u_ref[es, :], hb_s[...], (((1,), (1,)), ((), ())),
                                       preferred_element_type=F32)
        for il in range(pc * PEER_PIECE // PEER_KEYS, (pc + 1) * PEER_PIECE // PEER_KEYS):
            bs = slice(il * PEER_KEYS, (il + 1) * PEER_KEYS)
            for cb in range(ncol):
                ls = slice(cb * LANES, (cb + 1) * LANES)
                gate = jnp.zeros((PEER_KEYS, LANES), gdt)
                for h in range(PEER_HEADS):
                    rs = slice(h * PEER_KEYS, (h + 1) * PEER_KEYS)
                    cnt_row = rows_s[0, h * nblk + il:h * nblk + il + 1, ls].astype(gdt)
                    a0_row = rows_s[1, h * nblk + il:h * nblk + il + 1, ls].astype(gdt)
                    sel = r1_s[rs, ls] < cnt_row
                    gate = gate + jnp.where(sel, e1_s[rs, ls], jnp.zeros((), gdt)) * a0_row
                hu = hut_s[bs, ls]
                act = 0.5 * hu * (1.0 + lax.erf(hu * sqrt_half))
                coef_s[bs, ls] = (gate * act.astype(gdt)).astype(coef_s.dtype)
        acc_s[...] += jnp.dot(vt_ref[:, es], coef_s[es, :], preferred_element_type=F32)

    @pl.when(c == nc - 1)
    def _():
        o_ref[...] = x_ref[...] + acc_s[...].T


def _peer(x, lw, *, tm, ec):
    n = x.shape[0]
    assert n % tm == 0 and tm % MXU_TILE == 0 and ec % (PEER_KEYS * SUBLANES) == 0 and ec % PEER_PIECE == 0
    tok = pl.BlockSpec((tm, D_MODEL), lambda i, c: (i, 0))
    nq = PEER_HEADS * 2 * PEER_KEYS
    ncol = tm // LANES
    nkeys = PEER_HEADS * PEER_KEYS
    return pl.pallas_call(
        functools.partial(_peer_kernel, tm=tm, ec=ec),
        grid=(n // tm, N_EXPERTS // ec),
        in_specs=[tok,
                  pl.BlockSpec((1, D_MODEL), lambda i, c: (0, 0)),
                  pl.BlockSpec((nq, D_MODEL), lambda i, c: (0, 0), pipeline_mode=pl.Buffered(1)),
                  pl.BlockSpec((PEER_HEADS * 2, PEER_KEYS, PEER_KEYS), lambda i, c: (0, 0, 0),
                               pipeline_mode=pl.Buffered(1)),
                  pl.BlockSpec((ec, D_MODEL), lambda i, c: (c, 0)),
                  pl.BlockSpec((D_MODEL, ec), lambda i, c: (0, c))],
        out_specs=tok,
        out_shape=jax.ShapeDtypeStruct((n, D_MODEL), F32),
        scratch_shapes=[
            pltpu.VMEM((tm, D_MODEL), BF),
            pltpu.VMEM((nq, tm), BF),
            pltpu.VMEM((2, PEER_KEYS, tm), F32),
            pltpu.VMEM((nkeys, tm), BF),
            pltpu.VMEM((nkeys, tm), BF),
            pltpu.VMEM((nkeys, tm), F32),
            pltpu.VMEM((nkeys, tm), F32),
            pltpu.VMEM((2, PEER_HEADS * (ec // PEER_KEYS), tm), F32),
            pltpu.VMEM((ec, tm), F32),
            pltpu.VMEM((ec, tm), BF),
            pltpu.VMEM((D_MODEL, tm), F32),
        ],
        compiler_params=pltpu.CompilerParams(
            dimension_semantics=("arbitrary", "arbitrary"), vmem_limit_bytes=VMEM_LIMIT),
        name="peer",
    )(x, lw["norm_ffn"], lw["wq_t"], lw["keys"], lw["u"], lw["v_t"])


def _block_diag_pack(w):
    per = MXU_TILE // RNN_BLOCK
    nsup = w.shape[0] // per
    out = jnp.zeros((nsup, MXU_TILE, MXU_TILE), w.dtype)
    for s in range(nsup):
        for j in range(per):
            out = out.at[s, j * RNN_BLOCK:(j + 1) * RNN_BLOCK, j * RNN_BLOCK:(j + 1) * RNN_BLOCK].set(w[s * per + j])
    return out


def _layer_weights(l, norm_mix, norm_ffn, w_in, conv_w, conv_b, lru_wa, lru_ba, lru_wx, lru_bx, lru_lambda,
                   q_norm, k_norm, attn_sinks, pool_w, pool_scale, gate_b, w_branch, w_out,
                   peer_wq, peer_keys, peer_u, peer_v):
    w = w_in[l]
    o_q = 2 * D_RNN
    o_k = o_q + D_MODEL
    o_v = o_k + D_KV
    o_p = o_v + D_KV
    o_g = o_p + D_POOL
    ones_bd = jnp.asarray(np.kron(np.eye(MXU_TILE // HEAD_DIM), np.ones((HEAD_DIM, HEAD_DIM))), BF)
    return {
        "norm_mix": norm_mix[l][None, :],
        "norm_ffn": norm_ffn[l][None, :],
        "w_seq": jnp.concatenate([w[:, :o_q], w[:, o_p:o_g], w[:, o_k:o_p]], axis=1).astype(BF),
        "w_q": w[:, o_q:o_k].astype(BF),
        "w_g": w[:, o_g:].astype(BF),
        "conv_w": conv_w[l],
        "conv_b": conv_b[l][None, :],
        "wa": _block_diag_pack(lru_wa[l]).astype(BF),
        "wx": _block_diag_pack(lru_wx[l]).astype(BF),
        "ba": lru_ba[l][None, :],
        "bx": lru_bx[l][None, :],
        "lam": lru_lambda[l][None, :],
        "q_norm": jnp.tile(q_norm[l], N_HEADS)[None, :],
        "k_norm": jnp.tile(k_norm[l], N_KV)[None, :],
        "ones_bd": ones_bd,
        "sinks": attn_sinks[l],
        "pool_w": pool_w[l].astype(BF),
        "pool_scale": pool_scale[l][None, :],
        "gate_b": gate_b[l].reshape(1, N_BRANCH * D_MODEL),
        "w_branch": w_branch[l].astype(BF),
        "w_out": w_out[l].astype(BF),
        "wq_t": peer_wq[l].T.astype(BF),
        "keys": peer_keys[l].reshape(PEER_HEADS * 2, PEER_KEYS, PEER_KEYS).astype(BF),
        "u": peer_u[l].astype(BF),
        "v_t": peer_v[l].T.astype(BF),
    }


def kernel(x_prompt, x_sample, state_rglru_h, state_rglru_conv, cache_meta_k, cache_meta_v, cache_win_k, cache_win_v, state_pool, meta_tokens, norm_mix, norm_ffn, w_in, conv_w, conv_b, lru_wa, lru_ba, lru_wx, lru_bx, lru_lambda, q_norm, k_norm, attn_sinks, pool_w, pool_scale, gate_b, w_branch, w_out, peer_wq, peer_keys, peer_u, peer_v):
    bp, seq, _ = x_prompt.shape
    bs, ts, _ = x_sample.shape
    depth = w_in.shape[0]
    dt = x_prompt.dtype
    s_pad = PROMPT_PAD + N_META + seq
    nchunks = s_pad // CHUNK
    cpt = 3 if nchunks % 3 == 0 else 1
    tt_p = cpt * CHUNK

    xp = jnp.concatenate([jnp.zeros((bp, PROMPT_PAD, D_MODEL), dt),
                          jnp.broadcast_to(meta_tokens.astype(dt)[None], (bp, N_META, D_MODEL)), x_prompt], axis=1)
    n_p = bp * s_pad
    n_s = bs * ts
    x = jnp.concatenate([xp.reshape(n_p, D_MODEL), x_sample.reshape(n_s, D_MODEL)], axis=0)
    tm = 2 * n_s
    ec = 2048

    zeros_ch = jnp.zeros((bp, SUBLANES, D_RNN), dt)
    zeros_h = jnp.zeros((bp, 1, D_RNN), dt)
    zeros_ph = jnp.zeros((bp, POOL_MAX, D_POOL), dt)

    outs = {k: [] for k in ("ph", "pc", "pmk", "pmv", "pwk", "pwv", "ppool", "sh", "sc", "sk", "sv", "spool")}
    for l in range(depth):
        lw = _layer_weights(l, norm_mix, norm_ffn, w_in, conv_w, conv_b, lru_wa, lru_ba, lru_wx, lru_bx,
                            lru_lambda, q_norm, k_norm, attn_sinks, pool_w, pool_scale, gate_b, w_branch,
                            w_out, peer_wq, peer_keys, peer_u, peer_v)
        ya, yc, k, v, cs, hl, pst = _seq_mixer(x, lw, zeros_ch, zeros_h, zeros_ph, nb=bp, nt=s_pad // tt_p,
                                               tt=tt_p, row0=0, pad=PROMPT_PAD, pool_hist=0)
        yb = _attn_prompt(x, k, v, lw, nb=bp, cpt=cpt)
        ys_prompt = (ya, yb, yc)
        k4 = k.reshape(bp, s_pad, N_KV, HEAD_DIM)
        v4 = v.reshape(bp, s_pad, N_KV, HEAD_DIM)
        outs["ph"].append(hl[:, 0])
        outs["pc"].append(cs[:, SUBLANES - (CONV_W - 1):])
        outs["pmk"].append(k4[:, PROMPT_PAD:PROMPT_PAD + N_META])
        outs["pmv"].append(v4[:, PROMPT_PAD:PROMPT_PAD + N_META])
        outs["pwk"].append(k4[:, -WINDOW:])
        outs["pwv"].append(v4[:, -WINDOW:])
        outs["ppool"].append(pst[:, 1:])
        ch8 = jnp.pad(state_rglru_conv[l], ((0, 0), (SUBLANES - (CONV_W - 1), 0), (0, 0)))
        ph16 = jnp.pad(state_pool[l], ((0, 0), (1, 0), (0, 0)))
        ya, yc, k, v, cs, hl, pst = _seq_mixer(x, lw, ch8, state_rglru_h[l][:, None, :], ph16, nb=bs, nt=1,
                                               tt=ts, row0=n_p, pad=0, pool_hist=POOL_MAX - 1)
        yb = _attn_sample(x, k, v, cache_meta_k[l].reshape(bs, N_META, D_KV),
                          cache_meta_v[l].reshape(bs, N_META, D_KV),
                          cache_win_k[l].reshape(bs, WINDOW, D_KV), cache_win_v[l].reshape(bs, WINDOW, D_KV),
                          lw, nb=bs, t_new=ts, row0=n_p)
        x = _merge(x, ys_prompt, (ya, yb, yc), lw, tm=tm)
        x = _peer(x, lw, tm=tm, ec=ec)
        outs["sh"].append(hl[:, 0])
        outs["sc"].append(cs[:, SUBLANES - (CONV_W - 1):])
        outs["sk"].append(k.reshape(bs, ts, N_KV, HEAD_DIM))
        outs["sv"].append(v.reshape(bs, ts, N_KV, HEAD_DIM))
        outs["spool"].append(pst[:, 1:])

    y_prompt = x[:n_p].reshape(bp, s_pad, D_MODEL)[:, PROMPT_PAD + N_META:]
    y_sample = x[n_p:].reshape(bs, ts, D_MODEL)
    st = lambda key: jnp.stack(outs[key])
    return (y_prompt, y_sample, st("ph"), st("pc"), st("pmk"), st("pmv"), st("pwk"), st("pwv"), st("ppool"),
            st("sh"), st("sc"), st("sk"), st("sv"), st("spool"))
```

```python
import functools
import math

import jax
import jax.numpy as jnp
import numpy as np
from jax import lax
from jax.experimental import pallas as pl
from jax.experimental.pallas import tpu as pltpu

D_MODEL = 1024
CHUNK = 64
N_META = 16
EPS = 1e-6
D_RNN = 1024
RNN_BLOCK = 64
CONV_W = 4
LRU_C = 8.0
HEAD_DIM = 64
N_HEADS = 16
N_KV = 4
GQA_GROUP = N_HEADS // N_KV
D_KV = N_KV * HEAD_DIM
WINDOW = 128
D_POOL = 1024
POOL_SIZES = (2, 4, 8, 16)
POOL_GROUP = 256
POOL_MAX = 16
N_BRANCH = 3
PEER_HEADS = 8
PEER_KEYS = 128
N_EXPERTS = PEER_KEYS * PEER_KEYS
PEER_TOPK = 16
PEER_PIECE = 512
NEG_INF = -1e30

LANES = 128
SUBLANES = 8
MXU_TILE = 256
VMEM_LIMIT = 56 * 1024 * 1024

PROMPT_PAD = CHUNK - N_META
SEQ_COLS = 2 * D_RNN + D_POOL + 2 * D_KV

BF = jnp.bfloat16
F32 = jnp.float32


def _rmsnorm(x, g):
    xf = x.astype(F32)
    return xf * lax.rsqrt(jnp.mean(xf * xf, axis=-1, keepdims=True) + EPS) * g


def _head_sumsq(x, ones_bd):
    sq = x * x
    hi = sq.astype(BF)
    lo = (sq - hi.astype(F32)).astype(BF)
    out = []
    for c in range(x.shape[1] // MXU_TILE):
        sl = slice(c * MXU_TILE, (c + 1) * MXU_TILE)
        out.append(jnp.dot(hi[:, sl], ones_bd, preferred_element_type=F32)
                   + jnp.dot(lo[:, sl], ones_bd, preferred_element_type=F32))
    return out[0] if len(out) == 1 else jnp.concatenate(out, axis=1)


def _head_rmsnorm(x, g, ones_bd):
    ms = _head_sumsq(x, ones_bd) * (1.0 / HEAD_DIM)
    return x * lax.rsqrt(ms + EPS) * g


def _shift_rows(x, d, fill, rows):
    return jnp.where(rows < d, fill, pltpu.roll(x, d, axis=0))


def _seq_kernel(x_ref, g_ref, w_ref, cw_ref, cb_ref, wa_ref, wx_ref, ba_ref, bx_ref, lam_ref,
                kn_ref, ones_ref, pw_ref, ps_ref, ch_ref, h0_ref, ph_ref,
                ya_ref, yc_ref, k_ref, v_ref, cs_ref, hl_ref, pst_ref,
                exr, exp_, hc, *, tt, pad, pool_hist):
    t = pl.program_id(1)

    @pl.when(t == 0)
    def _():
        exr[0:SUBLANES, :] = ch_ref[0]
        exp_[0:POOL_MAX, :] = ph_ref[0]
        hc[...] = jnp.broadcast_to(h0_ref[0], hc.shape)

    hb = _rmsnorm(x_ref[...], g_ref[...]).astype(BF)
    z = jnp.dot(hb, w_ref[...], preferred_element_type=F32)
    xr = z[:, 0:D_RNN]
    gr = z[:, D_RNN:2 * D_RNN]
    xp = z[:, 2 * D_RNN:2 * D_RNN + D_POOL]
    kk = z[:, 2 * D_RNN + D_POOL:2 * D_RNN + D_POOL + D_KV]
    vv = z[:, 2 * D_RNN + D_POOL + D_KV:]

    rows = lax.broadcasted_iota(jnp.int32, (tt, 1), 0)
    pos = rows + (t * tt - pad)

    exr[SUBLANES:SUBLANES + tt, :] = xr
    xc = cb_ref[...] + cw_ref[3:4, :] * xr
    for j in range(CONV_W - 1):
        xc = xc + cw_ref[j:j + 1, :] * exr[SUBLANES - (CONV_W - 1) + j:SUBLANES - (CONV_W - 1) + j + tt, :]
    exr[0:SUBLANES, :] = exr[tt:tt + SUBLANES, :]
    cs_ref[0] = exr[0:SUBLANES, :]

    ra, ia = [], []
    for c in range(D_RNN // MXU_TILE):
        xcb = xc[:, c * MXU_TILE:(c + 1) * MXU_TILE].astype(BF)
        ra.append(jnp.dot(xcb, wa_ref[c], preferred_element_type=F32))
        ia.append(jnp.dot(xcb, wx_ref[c], preferred_element_type=F32))
    r = jax.nn.sigmoid(jnp.concatenate(ra, axis=1) + ba_ref[...])
    i = jax.nn.sigmoid(jnp.concatenate(ia, axis=1) + bx_ref[...])
    nl = -lam_ref[...]
    softplus = jnp.maximum(nl, 0.0) + jnp.log1p(jnp.exp(-jnp.abs(nl)))
    log_a = (-LRU_C * softplus) * r
    a = jnp.exp(log_a)
    u = jnp.sqrt(1.0 - jnp.exp(2.0 * log_a)) * (i * xc)
    if pad:
        u = jnp.where(pos >= 0, u, 0.0)

    sub = rows % SUBLANES
    d = 1
    while d < SUBLANES:
        a_s = _shift_rows(a, d, 1.0, sub)
        u_s = _shift_rows(u, d, 0.0, sub)
        u = a * u_s + u
        a = a * a_s
        d *= 2
    carry = hc[0:1, :]
    groups = []
    for gi in range(tt // SUBLANES):
        hg = a[gi * SUBLANES:(gi + 1) * SUBLANES, :] * carry + u[gi * SUBLANES:(gi + 1) * SUBLANES, :]
        groups.append(hg)
        carry = hg[SUBLANES - 1:SUBLANES, :]
    h = jnp.concatenate(groups, axis=0)
    hc[...] = jnp.broadcast_to(h[tt - 1:tt, :], hc.shape)
    hl_ref[0] = h[tt - 1:tt, :]
    ya_ref[...] = (h * jax.nn.gelu(gr, approximate=True)).astype(ya_ref.dtype)

    k_ref[...] = _head_rmsnorm(kk, kn_ref[...], ones_ref[...])
    v_ref[...] = vv

    exp_[POOL_MAX:POOL_MAX + tt, :] = xp
    ext = exp_[...]
    yc = []
    for gi, size in enumerate(POOL_SIZES):
        sl = slice(gi * POOL_GROUP, (gi + 1) * POOL_GROUP)
        s = ext[:, sl]
        step = 1
        while step < size:
            s = s + pltpu.roll(s, step, axis=0)
            step *= 2
        cnt = jnp.clip(pos + (pool_hist + 1), 1, size).astype(F32)
        pooled = s[POOL_MAX:, :] / cnt - xp[:, sl]
        yc.append(jnp.dot(pooled.astype(BF), pw_ref[gi], preferred_element_type=F32))
    yc_ref[...] = (jnp.concatenate(yc, axis=1) * ps_ref[...]).astype(yc_ref.dtype)
    exp_[0:POOL_MAX, :] = exp_[tt:tt + POOL_MAX, :]
    pst_ref[0] = exp_[0:POOL_MAX, :]


def _seq_mixer(x, lw, conv_hist8, h0, pool_hist16, *, nb, nt, tt, row0, pad, pool_hist):
    n = nb * nt * tt
    assert row0 % tt == 0
    xrow = lambda b, t: (row0 // tt + b * nt + t, 0)
    row = lambda b, t: (b * nt + t, 0)
    full2 = lambda b, t: (0, 0)
    full3 = lambda b, t: (0, 0, 0)
    per_b = lambda b, t: (b, 0, 0)
    kern = functools.partial(_seq_kernel, tt=tt, pad=pad, pool_hist=pool_hist)
    return pl.pallas_call(
        kern,
        grid=(nb, nt),
        in_specs=[
            pl.BlockSpec((tt, D_MODEL), xrow),
            pl.BlockSpec((1, D_MODEL), full2),
            pl.BlockSpec((D_MODEL, SEQ_COLS), full2),
            pl.BlockSpec((CONV_W, D_RNN), full2),
            pl.BlockSpec((1, D_RNN), full2),
            pl.BlockSpec((D_RNN // MXU_TILE, MXU_TILE, MXU_TILE), full3),
            pl.BlockSpec((D_RNN // MXU_TILE, MXU_TILE, MXU_TILE), full3),
            pl.BlockSpec((1, D_RNN), full2),
            pl.BlockSpec((1, D_RNN), full2),
            pl.BlockSpec((1, D_RNN), full2),
            pl.BlockSpec((1, D_KV), full2),
            pl.BlockSpec((MXU_TILE, MXU_TILE), full2),
            pl.BlockSpec((len(POOL_SIZES), POOL_GROUP, POOL_GROUP), full3),
            pl.BlockSpec((1, D_POOL), full2),
            pl.BlockSpec((1, SUBLANES, D_RNN), per_b),
            pl.BlockSpec((1, 1, D_RNN), per_b),
            pl.BlockSpec((1, POOL_MAX, D_POOL), per_b),
        ],
        out_specs=[
            pl.BlockSpec((tt, D_RNN), row),
            pl.BlockSpec((tt, D_POOL), row),
            pl.BlockSpec((tt, D_KV), row),
            pl.BlockSpec((tt, D_KV), row),
            pl.BlockSpec((1, SUBLANES, D_RNN), per_b),
            pl.BlockSpec((1, 1, D_RNN), per_b),
            pl.BlockSpec((1, POOL_MAX, D_POOL), per_b),
        ],
        out_shape=[
            jax.ShapeDtypeStruct((n, D_RNN), BF),
            jax.ShapeDtypeStruct((n, D_POOL), BF),
            jax.ShapeDtypeStruct((n, D_KV), F32),
            jax.ShapeDtypeStruct((n, D_KV), F32),
            jax.ShapeDtypeStruct((nb, SUBLANES, D_RNN), F32),
            jax.ShapeDtypeStruct((nb, 1, D_RNN), F32),
            jax.ShapeDtypeStruct((nb, POOL_MAX, D_POOL), F32),
        ],
        scratch_shapes=[
            pltpu.VMEM((tt + SUBLANES, D_RNN), F32),
            pltpu.VMEM((tt + POOL_MAX, D_POOL), F32),
            pltpu.VMEM((SUBLANES, D_RNN), F32),
        ],
        compiler_params=pltpu.CompilerParams(
            dimension_semantics=("arbitrary", "arbitrary"), vmem_limit_bytes=VMEM_LIMIT),
        name="seq_mixer",
    )(x, lw["norm_mix"], lw["w_seq"], lw["conv_w"], lw["conv_b"], lw["wa"], lw["wx"], lw["ba"], lw["bx"],
      lw["lam"], lw["k_norm"], lw["ones_bd"], lw["pool_w"], lw["pool_scale"], conv_hist8, h0, pool_hist16)


def _attend(q4, kcat, vcat, sink_col, col_valid):
    lg = lax.dot_general(q4, kcat, (((1,), (1,)), ((), ())), preferred_element_type=F32) * (HEAD_DIM ** -0.5)
    if col_valid is not None:
        lg = jnp.where(col_valid, lg, NEG_INF)
    m = jnp.maximum(jnp.max(lg, axis=-1, keepdims=True), sink_col)
    e = jnp.exp(lg - m)
    p = e / (jnp.sum(e, axis=-1, keepdims=True) + jnp.exp(sink_col - m))
    return jnp.dot(p.astype(BF), vcat, preferred_element_type=F32)


def _project_q(x_ref, g_ref, wq_ref, qn_ref, ones_ref):
    hb = _rmsnorm(x_ref[...], g_ref[...]).astype(BF)
    q = jnp.dot(hb, wq_ref[...], preferred_element_type=F32)
    return _head_rmsnorm(q, qn_ref[...], ones_ref[...]).astype(BF)


def _sink_column(sink_ref, kh, rows_per_head):
    r = lax.broadcasted_iota(jnp.int32, (GQA_GROUP * rows_per_head, 1), 0)
    col = jnp.full((GQA_GROUP * rows_per_head, 1), sink_ref[kh * GQA_GROUP], F32)
    for gq in range(1, GQA_GROUP):
        col = jnp.where(r >= gq * rows_per_head, sink_ref[kh * GQA_GROUP + gq], col)
    return col


def _attn_prompt_kernel(sink_ref, x_ref, g_ref, wq_ref, qn_ref, ones_ref,
                        kc_ref, kh1_ref, kh2_ref, km_ref, vc_ref, vh1_ref, vh2_ref, vm_ref,
                        yb_ref, *, cpt):
    g = pl.program_id(1)
    q = _project_q(x_ref, g_ref, wq_ref, qn_ref, ones_ref)
    kwin = jnp.concatenate([kh1_ref[...], kh2_ref[...], kc_ref[...]], axis=0).astype(BF)
    vwin = jnp.concatenate([vh1_ref[...], vh2_ref[...], vc_ref[...]], axis=0).astype(BF)
    kmeta = km_ref[PROMPT_PAD:CHUNK, :].astype(BF)
    vmeta = vm_ref[PROMPT_PAD:CHUNK, :].astype(BF)
    n_keys = N_META + 3 * CHUNK
    col = lax.broadcasted_iota(jnp.int32, (1, n_keys), 1)
    for j in range(cpt):
        c = g * cpt + j
        first = N_META + CHUNK * jnp.clip(3 - c, 0, 3)
        first = jnp.where(c >= 1, first, n_keys)
        valid = (col < N_META) | (col >= first)
        for kh in range(N_KV):
            hs = slice(kh * HEAD_DIM, (kh + 1) * HEAD_DIM)
            kcat = jnp.concatenate([kmeta[:, hs], kwin[j * CHUNK:(j + 3) * CHUNK, hs]], axis=0)
            vcat = jnp.concatenate([vmeta[:, hs], vwin[j * CHUNK:(j + 3) * CHUNK, hs]], axis=0)
            q4 = jnp.concatenate(
                [q[j * CHUNK:(j + 1) * CHUNK, (kh * GQA_GROUP + gq) * HEAD_DIM:(kh * GQA_GROUP + gq + 1) * HEAD_DIM]
                 for gq in range(GQA_GROUP)], axis=0)
            o = _attend(q4, kcat, vcat, _sink_column(sink_ref, kh, CHUNK), valid)
            for gq in range(GQA_GROUP):
                hq = kh * GQA_GROUP + gq
                yb_ref[j * CHUNK:(j + 1) * CHUNK, hq * HEAD_DIM:(hq + 1) * HEAD_DIM] = (
                    o[gq * CHUNK:(gq + 1) * CHUNK, :].astype(yb_ref.dtype))


def _attn_prompt(x, k, v, lw, *, nb, cpt):
    n = k.shape[0]
    nchunks = n // (nb * CHUNK)
    nt = nchunks // cpt
    tt = cpt * CHUNK
    row = lambda b, g: (b * nt + g, 0)
    full2 = lambda b, g: (0, 0)
    halo1 = lambda b, g: (b * nchunks + jnp.maximum(g * cpt - 2, 0), 0)
    halo2 = lambda b, g: (b * nchunks + jnp.maximum(g * cpt - 1, 0), 0)
    meta = lambda b, g: (b * nchunks, 0)
    kv_specs = [pl.BlockSpec((tt, D_KV), row), pl.BlockSpec((CHUNK, D_KV), halo1),
                pl.BlockSpec((CHUNK, D_KV), halo2), pl.BlockSpec((CHUNK, D_KV), meta)]
    return pl.pallas_call(
        functools.partial(_attn_prompt_kernel, cpt=cpt),
        grid=(nb, nt),
        in_specs=[
            pl.BlockSpec(memory_space=pltpu.SMEM),
            pl.BlockSpec((tt, D_MODEL), row),
            pl.BlockSpec((1, D_MODEL), full2),
            pl.BlockSpec((D_MODEL, D_MODEL), full2),
            pl.BlockSpec((1, D_MODEL), full2),
            pl.BlockSpec((MXU_TILE, MXU_TILE), full2),
        ] + kv_specs + kv_specs,
        out_specs=pl.BlockSpec((tt, D_MODEL), row),
        out_shape=jax.ShapeDtypeStruct((n, D_MODEL), BF),
        compiler_params=pltpu.CompilerParams(
            dimension_semantics=("arbitrary", "arbitrary"), vmem_limit_bytes=VMEM_LIMIT),
        name="attn_prompt",
    )(lw["sinks"], x, lw["norm_mix"], lw["w_q"], lw["q_norm"], lw["ones_bd"], k, k, k, k, v, v, v, v)


def _attn_sample_kernel(sink_ref, x_ref, g_ref, wq_ref, qn_ref, ones_ref,
                        kn_ref, kw_ref, km_ref, vn_ref, vw_ref, vm_ref, yb_ref, *, t_new):
    q = _project_q(x_ref, g_ref, wq_ref, qn_ref, ones_ref)
    kall = jnp.concatenate([km_ref[0], kw_ref[0], kn_ref[...]], axis=0).astype(BF)
    vall = jnp.concatenate([vm_ref[0], vw_ref[0], vn_ref[...]], axis=0).astype(BF)
    for kh in range(N_KV):
        hs = slice(kh * HEAD_DIM, (kh + 1) * HEAD_DIM)
        q4 = jnp.concatenate(
            [q[:, (kh * GQA_GROUP + gq) * HEAD_DIM:(kh * GQA_GROUP + gq + 1) * HEAD_DIM]
             for gq in range(GQA_GROUP)], axis=0)
        o = _attend(q4, kall[:, hs], vall[:, hs], _sink_column(sink_ref, kh, t_new), None)
        for gq in range(GQA_GROUP):
            hq = kh * GQA_GROUP + gq
            yb_ref[:, hq * HEAD_DIM:(hq + 1) * HEAD_DIM] = o[gq * t_new:(gq + 1) * t_new, :].astype(yb_ref.dtype)


def _attn_sample(x, k, v, meta_k, meta_v, win_k, win_v, lw, *, nb, t_new, row0):
    n = nb * t_new
    assert row0 % t_new == 0
    xrow = lambda b: (row0 // t_new + b, 0)
    row = lambda b: (b, 0)
    full2 = lambda b: (0, 0)
    per_b = lambda b: (b, 0, 0)
    kv_specs = [pl.BlockSpec((t_new, D_KV), row), pl.BlockSpec((1, WINDOW, D_KV), per_b),
                pl.BlockSpec((1, N_META, D_KV), per_b)]
    return pl.pallas_call(
        functools.partial(_attn_sample_kernel, t_new=t_new),
        grid=(nb,),
        in_specs=[
            pl.BlockSpec(memory_space=pltpu.SMEM),
            pl.BlockSpec((t_new, D_MODEL), xrow),
            pl.BlockSpec((1, D_MODEL), full2),
            pl.BlockSpec((D_MODEL, D_MODEL), full2),
            pl.BlockSpec((1, D_MODEL), full2),
            pl.BlockSpec((MXU_TILE, MXU_TILE), full2),
        ] + kv_specs + kv_specs,
        out_specs=pl.BlockSpec((t_new, D_MODEL), row),
        out_shape=jax.ShapeDtypeStruct((n, D_MODEL), BF),
        compiler_params=pltpu.CompilerParams(
            dimension_semantics=("arbitrary",), vmem_limit_bytes=VMEM_LIMIT),
        name="attn_sample",
    )(lw["sinks"], x, lw["norm_mix"], lw["w_q"], lw["q_norm"], lw["ones_bd"], k, win_k, meta_k, v, win_v, meta_v)


def _merge_kernel(x_ref, *refs):
    y_refs, (g_ref, wg_ref, gb_ref, wb_ref, wo_ref, o_ref) = refs[:3 * N_BRANCH], refs[3 * N_BRANCH:]
    last = pl.program_id(0) == pl.num_programs(0) - 1
    x = x_ref[...]
    hb = _rmsnorm(x, g_ref[...]).astype(BF)
    merged = None
    for nbr in range(N_BRANCH):
        lo_ref, hi_ref, s_ref = y_refs[3 * nbr:3 * nbr + 3]
        y = jnp.concatenate([lo_ref[...], jnp.where(last, s_ref[...], hi_ref[...])], axis=0)
        gl = jnp.dot(hb, wg_ref[:, nbr * D_MODEL:(nbr + 1) * D_MODEL], preferred_element_type=F32)
        gate = jax.nn.sigmoid(gl + gb_ref[:, nbr * D_MODEL:(nbr + 1) * D_MODEL])
        term = gate * jnp.dot(y, wb_ref[nbr], preferred_element_type=F32)
        merged = term if merged is None else merged + term
    o_ref[...] = x + jnp.dot(merged.astype(BF), wo_ref[...], preferred_element_type=F32)


def _merge(x, ys_prompt, ys_sample, lw, *, tm):
    n = x.shape[0]
    half = tm // 2
    n_p = ys_prompt[0].shape[0]
    assert n % tm == 0 and n_p % tm == half and ys_sample[0].shape[0] == half
    last_half = n_p // half - 1
    row = lambda i: (i, 0)
    lo = lambda i: (jnp.minimum(2 * i, last_half), 0)
    hi = lambda i: (jnp.minimum(2 * i + 1, last_half), 0)
    full2 = lambda i: (0, 0)
    full3 = lambda i: (0, 0, 0)
    tok = pl.BlockSpec((tm, D_MODEL), row)
    y_specs, y_args = [], []
    for yp, ys in zip(ys_prompt, ys_sample):
        y_specs += [pl.BlockSpec((half, D_MODEL), lo), pl.BlockSpec((half, D_MODEL), hi),
                    pl.BlockSpec((half, D_MODEL), full2)]
        y_args += [yp, yp, ys]
    return pl.pallas_call(
        _merge_kernel,
        grid=(n // tm,),
        in_specs=[tok] + y_specs + [
                  pl.BlockSpec((1, D_MODEL), full2),
                  pl.BlockSpec((D_MODEL, N_BRANCH * D_MODEL), full2),
                  pl.BlockSpec((1, N_BRANCH * D_MODEL), full2),
                  pl.BlockSpec((N_BRANCH, D_MODEL, D_MODEL), full3),
                  pl.BlockSpec((D_MODEL, D_MODEL), full2)],
        out_specs=tok,
        out_shape=jax.ShapeDtypeStruct((n, D_MODEL), F32),
        compiler_params=pltpu.CompilerParams(
            dimension_semantics=("arbitrary",), vmem_limit_bytes=VMEM_LIMIT),
        name="merge",
    )(x, *y_args, lw["norm_mix"], lw["w_g"], lw["gate_b"], lw["w_branch"], lw["w_out"])


def _top_values(s, k, with_rank=False):
    vals = []
    cur = s
    rank = jnp.full(s.shape, float(k), F32)
    for i in range(k):
        m = jnp.max(cur, axis=0, keepdims=True)
        vals.append(m)
        eq = cur == m
        if with_rank:
            rank = jnp.where(eq, float(i), rank)
        cur = jnp.where(eq, -jnp.inf, cur)
    top = jnp.concatenate(vals, axis=0)
    return (top, rank) if with_rank else top


def _kth_largest(s, k):
    cur = s
    left = jnp.full((1, s.shape[1]), float(k), F32)
    thr = jnp.full((1, s.shape[1]), jnp.inf, F32)
    for _ in range(k):
        m = jnp.max(cur, axis=0, keepdims=True)
        eq = cur == m
        thr = jnp.where(left > 0.0, m, thr)
        left = left - jnp.sum(eq.astype(F32), axis=0, keepdims=True)
        cur = jnp.where(eq, -jnp.inf, cur)
    return thr


def _pruned_candidates(t0, t1):
    k = t0.shape[0]
    return jnp.concatenate([t0[a:a + 1, :] + t1[0:k // (a + 1), :] for a in range(k)], axis=0)


def _peer_kernel(x_ref, g_ref, wqt_ref, keys_ref, u_ref, vt_ref, o_ref,
                 hb_s, qt_s, sc_s, r1_s, e1_s, cnt_s, a0_s, rows_s, hut_s, coef_s, acc_s, *, tm, ec):
    c = pl.program_id(1)
    nc = pl.num_programs(1)
    nblk = ec // PEER_KEYS
    ncol = tm // LANES

    @pl.when(c == 0)
    def _():
        hb = _rmsnorm(x_ref[...], g_ref[...]).astype(BF)
        hb_s[...] = hb
        qt_s[...] = lax.dot_general(wqt_ref[...], hb, (((1,), (1,)), ((), ())),
                                    preferred_element_type=F32).astype(BF)
        for h in range(PEER_HEADS):
            for p in range(2):
                r0 = (h * 2 + p) * PEER_KEYS
                sc_s[p] = jnp.dot(keys_ref[h * 2 + p], qt_s[r0:r0 + PEER_KEYS, :], preferred_element_type=F32)
            rs = slice(h * PEER_KEYS, (h + 1) * PEER_KEYS)
            for cb in range(ncol):
                ls = slice(cb * LANES, (cb + 1) * LANES)
                s0 = sc_s[0, :, ls]
                s1 = sc_s[1, :, ls]
                t0 = _top_values(s0, PEER_TOPK)
                t1, rank1 = _top_values(s1, PEER_TOPK, with_rank=True)
                cand = _pruned_candidates(t0, t1)
                thr = _kth_largest(cand, PEER_TOPK)
                m0 = t0[0:1, :]
                m1 = t1[0:1, :]
                zsum = jnp.sum(jnp.where(cand >= thr, jnp.exp(cand - (m0 + m1)), 0.0), axis=0, keepdims=True)
                cnt = jnp.zeros_like(s0)
                for b in range(PEER_TOPK):
                    keep = (t0 + t1[b:b + 1, :]) >= thr
                    sigma = jnp.min(jnp.where(keep, t0, jnp.inf), axis=0, keepdims=True)
                    cnt = cnt + (s0 >= sigma).astype(F32)
                r1_s[rs, ls] = rank1.astype(r1_s.dtype)
                e1_s[rs, ls] = (jnp.exp(s1 - m1) / zsum).astype(e1_s.dtype)
                cnt_s[rs, ls] = cnt
                a0_s[rs, ls] = jnp.exp(s0 - m0)
        acc_s[...] = jnp.zeros_like(acc_s)


    for h in range(PEER_HEADS):
        grp = pl.multiple_of(h * PEER_KEYS + c * nblk, SUBLANES)
        rows_s[0, h * nblk:(h + 1) * nblk, :] = cnt_s[pl.ds(grp, nblk), :]
        rows_s[1, h * nblk:(h + 1) * nblk, :] = a0_s[pl.ds(grp, nblk), :]

    sqrt_half = math.sqrt(0.5)
    gdt = r1_s.dtype
    hut_s[...] = lax.dot_general(u_ref[...], hb_s[...], (((1,), (1,)), ((), ())), preferred_element_type=F32)

    def build(pc):
        for il in range(pc * PEER_PIECE // PEER_KEYS, (pc + 1) * PEER_PIECE // PEER_KEYS):
            bs = slice(il * PEER_KEYS, (il + 1) * PEER_KEYS)
            for cb in range(ncol):
                ls = slice(cb * LANES, (cb + 1) * LANES)
                gate = jnp.zeros((PEER_KEYS, LANES), gdt)
                for h in range(PEER_HEADS):
                    rs = slice(h * PEER_KEYS, (h + 1) * PEER_KEYS)
                    cnt_row = rows_s[0, h * nblk + il:h * nblk + il + 1, ls].astype(gdt)
                    a0_row = rows_s[1, h * nblk + il:h * nblk + il + 1, ls].astype(gdt)
                    sel = r1_s[rs, ls] < cnt_row
                    gate = gate + jnp.where(sel, e1_s[rs, ls], jnp.zeros((), gdt)) * a0_row
                hu = hut_s[bs, ls]
                act = 0.5 * hu * (1.0 + lax.erf(hu * sqrt_half))
                coef_s[bs, ls] = (gate * act.astype(gdt)).astype(coef_s.dtype)

    def accumulate(pc):
        es = slice(pc * PEER_PIECE, (pc + 1) * PEER_PIECE)
        acc_s[...] += jnp.dot(vt_ref[:, es], coef_s[es, :], preferred_element_type=F32)

    npc = ec // PEER_PIECE
    build(0)
    for pc in range(1, npc):
        build(pc)
        accumulate(pc - 1)
    accumulate(npc - 1)

    @pl.when(c == nc - 1)
    def _():
        o_ref[...] = x_ref[...] + acc_s[...].T


def _peer(x, lw, *, tm, ec):
    n = x.shape[0]
    assert n % tm == 0 and tm % MXU_TILE == 0 and ec % (PEER_KEYS * SUBLANES) == 0 and N_EXPERTS % ec == 0
    tok = pl.BlockSpec((tm, D_MODEL), lambda i, c: (i, 0))
    nq = PEER_HEADS * 2 * PEER_KEYS
    nkeys = PEER_HEADS * PEER_KEYS
    return pl.pallas_call(
        functools.partial(_peer_kernel, tm=tm, ec=ec),
        grid=(n // tm, N_EXPERTS // ec),
        in_specs=[tok,
                  pl.BlockSpec((1, D_MODEL), lambda i, c: (0, 0)),
                  pl.BlockSpec((nq, D_MODEL), lambda i, c: (0, 0), pipeline_mode=pl.Buffered(1)),
                  pl.BlockSpec((PEER_HEADS * 2, PEER_KEYS, PEER_KEYS), lambda i, c: (0, 0, 0),
                               pipeline_mode=pl.Buffered(1)),
                  pl.BlockSpec((ec, D_MODEL), lambda i, c: (c, 0)),
                  pl.BlockSpec((D_MODEL, ec), lambda i, c: (0, c))],
        out_specs=tok,
        out_shape=jax.ShapeDtypeStruct((n, D_MODEL), F32),
        scratch_shapes=[
            pltpu.VMEM((tm, D_MODEL), BF),
            pltpu.VMEM((nq, tm), BF),
            pltpu.VMEM((2, PEER_KEYS, tm), F32),
            pltpu.VMEM((nkeys, tm), BF),
            pltpu.VMEM((nkeys, tm), BF),
            pltpu.VMEM((nkeys, tm), F32),
            pltpu.VMEM((nkeys, tm), F32),
            pltpu.VMEM((2, PEER_HEADS * (ec // PEER_KEYS), tm), F32),
            pltpu.VMEM((ec, tm), F32),
            pltpu.VMEM((ec, tm), BF),
            pltpu.VMEM((D_MODEL, tm), F32),
        ],
        compiler_params=pltpu.CompilerParams(
            dimension_semantics=("arbitrary", "arbitrary"), vmem_limit_bytes=VMEM_LIMIT),
        name="peer",
    )(x, lw["norm_ffn"], lw["wq_t"], lw["keys"], lw["u"], lw["v_t"])


def _block_diag_pack(w):
    per = MXU_TILE // RNN_BLOCK
    nsup = w.shape[0] // per
    out = jnp.zeros((nsup, MXU_TILE, MXU_TILE), w.dtype)
    for s in range(nsup):
        for j in range(per):
            out = out.at[s, j * RNN_BLOCK:(j + 1) * RNN_BLOCK, j * RNN_BLOCK:(j + 1) * RNN_BLOCK].set(w[s * per + j])
    return out


def _layer_weights(l, norm_mix, norm_ffn, w_in, conv_w, conv_b, lru_wa, lru_ba, lru_wx, lru_bx, lru_lambda,
                   q_norm, k_norm, attn_sinks, pool_w, pool_scale, gate_b, w_branch, w_out,
                   peer_wq, peer_keys, peer_u, peer_v):
    w = w_in[l]
    o_q = 2 * D_RNN
    o_k = o_q + D_MODEL
    o_v = o_k + D_KV
    o_p = o_v + D_KV
    o_g = o_p + D_POOL
    ones_bd = jnp.asarray(np.kron(np.eye(MXU_TILE // HEAD_DIM), np.ones((HEAD_DIM, HEAD_DIM))), BF)
    return {
        "norm_mix": norm_mix[l][None, :],
        "norm_ffn": norm_ffn[l][None, :],
        "w_seq": jnp.concatenate([w[:, :o_q], w[:, o_p:o_g], w[:, o_k:o_p]], axis=1).astype(BF),
        "w_q": w[:, o_q:o_k].astype(BF),
        "w_g": w[:, o_g:].astype(BF),
        "conv_w": conv_w[l],
        "conv_b": conv_b[l][None, :],
        "wa": _block_diag_pack(lru_wa[l]).astype(BF),
        "wx": _block_diag_pack(lru_wx[l]).astype(BF),
        "ba": lru_ba[l][None, :],
        "bx": lru_bx[l][None, :],
        "lam": lru_lambda[l][None, :],
        "q_norm": jnp.tile(q_norm[l], N_HEADS)[None, :],
        "k_norm": jnp.tile(k_norm[l], N_KV)[None, :],
        "ones_bd": ones_bd,
        "sinks": attn_sinks[l],
        "pool_w": pool_w[l].astype(BF),
        "pool_scale": pool_scale[l][None, :],
        "gate_b": gate_b[l].reshape(1, N_BRANCH * D_MODEL),
        "w_branch": w_branch[l].astype(BF),
        "w_out": w_out[l].astype(BF),
        "wq_t": peer_wq[l].T.astype(BF),
        "keys": peer_keys[l].reshape(PEER_HEADS * 2, PEER_KEYS, PEER_KEYS).astype(BF),
        "u": peer_u[l].astype(BF),
        "v_t": peer_v[l].T.astype(BF),
    }


def kernel(x_prompt, x_sample, state_rglru_h, state_rglru_conv, cache_meta_k, cache_meta_v, cache_win_k, cache_win_v, state_pool, meta_tokens, norm_mix, norm_ffn, w_in, conv_w, conv_b, lru_wa, lru_ba, lru_wx, lru_bx, lru_lambda, q_norm, k_norm, attn_sinks, pool_w, pool_scale, gate_b, w_branch, w_out, peer_wq, peer_keys, peer_u, peer_v):
    bp, seq, _ = x_prompt.shape
    bs, ts, _ = x_sample.shape
    depth = w_in.shape[0]
    dt = x_prompt.dtype
    s_pad = PROMPT_PAD + N_META + seq
    nchunks = s_pad // CHUNK
    cpt = 3 if nchunks % 3 == 0 else 1
    tt_p = cpt * CHUNK

    xp = jnp.concatenate([jnp.zeros((bp, PROMPT_PAD, D_MODEL), dt),
                          jnp.broadcast_to(meta_tokens.astype(dt)[None], (bp, N_META, D_MODEL)), x_prompt], axis=1)
    n_p = bp * s_pad
    n_s = bs * ts
    x = jnp.concatenate([xp.reshape(n_p, D_MODEL), x_sample.reshape(n_s, D_MODEL)], axis=0)
    tm = 2 * n_s
    ec = 2048

    zeros_ch = jnp.zeros((bp, SUBLANES, D_RNN), dt)
    zeros_h = jnp.zeros((bp, 1, D_RNN), dt)
    zeros_ph = jnp.zeros((bp, POOL_MAX, D_POOL), dt)

    outs = {k: [] for k in ("ph", "pc", "pmk", "pmv", "pwk", "pwv", "ppool", "sh", "sc", "sk", "sv", "spool")}
    for l in range(depth):
        lw = _layer_weights(l, norm_mix, norm_ffn, w_in, conv_w, conv_b, lru_wa, lru_ba, lru_wx, lru_bx,
                            lru_lambda, q_norm, k_norm, attn_sinks, pool_w, pool_scale, gate_b, w_branch,
                            w_out, peer_wq, peer_keys, peer_u, peer_v)
        ya, yc, k, v, cs, hl, pst = _seq_mixer(x, lw, zeros_ch, zeros_h, zeros_ph, nb=bp, nt=s_pad // tt_p,
                                               tt=tt_p, row0=0, pad=PROMPT_PAD, pool_hist=0)
        yb = _attn_prompt(x, k, v, lw, nb=bp, cpt=cpt)
        ys_prompt = (ya, yb, yc)
        k4 = k.reshape(bp, s_pad, N_KV, HEAD_DIM)
        v4 = v.reshape(bp, s_pad, N_KV, HEAD_DIM)
        outs["ph"].append(hl[:, 0])
        outs["pc"].append(cs[:, SUBLANES - (CONV_W - 1):])
        outs["pmk"].append(k4[:, PROMPT_PAD:PROMPT_PAD + N_META])
        outs["pmv"].append(v4[:, PROMPT_PAD:PROMPT_PAD + N_META])
        outs["pwk"].append(k4[:, -WINDOW:])
        outs["pwv"].append(v4[:, -WINDOW:])
        outs["ppool"].append(pst[:, 1:])
        ch8 = jnp.pad(state_rglru_conv[l], ((0, 0), (SUBLANES - (CONV_W - 1), 0), (0, 0)))
        ph16 = jnp.pad(state_pool[l], ((0, 0), (1, 0), (0, 0)))
        ya, yc, k, v, cs, hl, pst = _seq_mixer(x, lw, ch8, state_rglru_h[l][:, None, :], ph16, nb=bs, nt=1,
                                               tt=ts, row0=n_p, pad=0, pool_hist=POOL_MAX - 1)
        yb = _attn_sample(x, k, v, cache_meta_k[l].reshape(bs, N_META, D_KV),
                          cache_meta_v[l].reshape(bs, N_META, D_KV),
                          cache_win_k[l].reshape(bs, WINDOW, D_KV), cache_win_v[l].reshape(bs, WINDOW, D_KV),
                          lw, nb=bs, t_new=ts, row0=n_p)
        x = _merge(x, ys_prompt, (ya, yb, yc), lw, tm=tm)
        x = _peer(x, lw, tm=tm, ec=ec)
        outs["sh"].append(hl[:, 0])
        outs["sc"].append(cs[:, SUBLANES - (CONV_W - 1):])
        outs["sk"].append(k.reshape(bs, ts, N_KV, HEAD_DIM))
        outs["sv"].append(v.reshape(bs, ts, N_KV, HEAD_DIM))
        outs["spool"].append(pst[:, 1:])

    y_prompt = x[:n_p].reshape(bp, s_pad, D_MODEL)[:, PROMPT_PAD + N_META:]
    y_sample = x[n_p:].reshape(bs, ts, D_MODEL)
    st = lambda key: jnp.stack(outs[key])
    return (y_prompt, y_sample, st("ph"), st("pc"), st("pmk"), st("pmv"), st("pwk"), st("pwv"), st("ppool"),
            st("sh"), st("sc"), st("sk"), st("sv"), st("spool"))
```

```python
import functools
import math

import jax
import jax.numpy as jnp
import numpy as np
from jax import lax
from jax.experimental import pallas as pl
from jax.experimental.pallas import tpu as pltpu

D_MODEL = 1024
CHUNK = 64
N_META = 16
EPS = 1e-6
D_RNN = 1024
RNN_BLOCK = 64
CONV_W = 4
LRU_C = 8.0
HEAD_DIM = 64
N_HEADS = 16
N_KV = 4
GQA_GROUP = N_HEADS // N_KV
D_KV = N_KV * HEAD_DIM
WINDOW = 128
D_POOL = 1024
POOL_SIZES = (2, 4, 8, 16)
POOL_GROUP = 256
POOL_MAX = 16
N_BRANCH = 3
PEER_HEADS = 8
PEER_KEYS = 128
N_EXPERTS = PEER_KEYS * PEER_KEYS
PEER_TOPK = 16
PEER_PIECE = 512
NEG_INF = -1e30

LANES = 128
SUBLANES = 8
MXU_TILE = 256
VMEM_LIMIT = 56 * 1024 * 1024

PROMPT_PAD = CHUNK - N_META
SEQ_COLS = 2 * D_RNN + D_POOL + 2 * D_KV

BF = jnp.bfloat16
F32 = jnp.float32


def _rmsnorm(x, g):
    xf = x.astype(F32)
    return xf * lax.rsqrt(jnp.mean(xf * xf, axis=-1, keepdims=True) + EPS) * g


def _head_sumsq(x, ones_bd):
    sq = x * x
    hi = sq.astype(BF)
    lo = (sq - hi.astype(F32)).astype(BF)
    out = []
    for c in range(x.shape[1] // MXU_TILE):
        sl = slice(c * MXU_TILE, (c + 1) * MXU_TILE)
        out.append(jnp.dot(hi[:, sl], ones_bd, preferred_element_type=F32)
                   + jnp.dot(lo[:, sl], ones_bd, preferred_element_type=F32))
    return out[0] if len(out) == 1 else jnp.concatenate(out, axis=1)


def _head_rmsnorm(x, g, ones_bd):
    ms = _head_sumsq(x, ones_bd) * (1.0 / HEAD_DIM)
    return x * lax.rsqrt(ms + EPS) * g


def _shift_rows(x, d, fill, rows):
    return jnp.where(rows < d, fill, pltpu.roll(x, d, axis=0))


def _seq_kernel(x_ref, g_ref, w_ref, cw_ref, cb_ref, wa_ref, wx_ref, ba_ref, bx_ref, lam_ref,
                kn_ref, ones_ref, pw_ref, ps_ref, ch_ref, h0_ref, ph_ref,
                ya_ref, yc_ref, k_ref, v_ref, cs_ref, hl_ref, pst_ref,
                exr, exp_, hc, *, tt, pad, pool_hist):
    t = pl.program_id(1)

    @pl.when(t == 0)
    def _():
        exr[0:SUBLANES, :] = ch_ref[0]
        exp_[0:POOL_MAX, :] = ph_ref[0]
        hc[...] = jnp.broadcast_to(h0_ref[0], hc.shape)

    hb = _rmsnorm(x_ref[...], g_ref[...]).astype(BF)
    z = jnp.dot(hb, w_ref[...], preferred_element_type=F32)
    xr = z[:, 0:D_RNN]
    gr = z[:, D_RNN:2 * D_RNN]
    xp = z[:, 2 * D_RNN:2 * D_RNN + D_POOL]
    kk = z[:, 2 * D_RNN + D_POOL:2 * D_RNN + D_POOL + D_KV]
    vv = z[:, 2 * D_RNN + D_POOL + D_KV:]

    rows = lax.broadcasted_iota(jnp.int32, (tt, 1), 0)
    pos = rows + (t * tt - pad)

    exr[SUBLANES:SUBLANES + tt, :] = xr
    xc = cb_ref[...] + cw_ref[3:4, :] * xr
    for j in range(CONV_W - 1):
        xc = xc + cw_ref[j:j + 1, :] * exr[SUBLANES - (CONV_W - 1) + j:SUBLANES - (CONV_W - 1) + j + tt, :]
    exr[0:SUBLANES, :] = exr[tt:tt + SUBLANES, :]
    cs_ref[0] = exr[0:SUBLANES, :]

    ra, ia = [], []
    for c in range(D_RNN // MXU_TILE):
        xcb = xc[:, c * MXU_TILE:(c + 1) * MXU_TILE].astype(BF)
        ra.append(jnp.dot(xcb, wa_ref[c], preferred_element_type=F32))
        ia.append(jnp.dot(xcb, wx_ref[c], preferred_element_type=F32))
    r = jax.nn.sigmoid(jnp.concatenate(ra, axis=1) + ba_ref[...])
    i = jax.nn.sigmoid(jnp.concatenate(ia, axis=1) + bx_ref[...])
    nl = -lam_ref[...]
    softplus = jnp.maximum(nl, 0.0) + jnp.log1p(jnp.exp(-jnp.abs(nl)))
    log_a = (-LRU_C * softplus) * r
    a = jnp.exp(log_a)
    u = jnp.sqrt(1.0 - jnp.exp(2.0 * log_a)) * (i * xc)
    if pad:
        u = jnp.where(pos >= 0, u, 0.0)

    sub = rows % SUBLANES
    d = 1
    while d < SUBLANES:
        a_s = _shift_rows(a, d, 1.0, sub)
        u_s = _shift_rows(u, d, 0.0, sub)
        u = a * u_s + u
        a = a * a_s
        d *= 2
    carry = hc[0:1, :]
    groups = []
    for gi in range(tt // SUBLANES):
        hg = a[gi * SUBLANES:(gi + 1) * SUBLANES, :] * carry + u[gi * SUBLANES:(gi + 1) * SUBLANES, :]
        groups.append(hg)
        carry = hg[SUBLANES - 1:SUBLANES, :]
    h = jnp.concatenate(groups, axis=0)
    hc[...] = jnp.broadcast_to(h[tt - 1:tt, :], hc.shape)
    hl_ref[0] = h[tt - 1:tt, :]
    ya_ref[...] = (h * jax.nn.gelu(gr, approximate=True)).astype(ya_ref.dtype)

    k_ref[...] = _head_rmsnorm(kk, kn_ref[...], ones_ref[...])
    v_ref[...] = vv

    exp_[POOL_MAX:POOL_MAX + tt, :] = xp
    ext = exp_[...]
    yc = []
    for gi, size in enumerate(POOL_SIZES):
        sl = slice(gi * POOL_GROUP, (gi + 1) * POOL_GROUP)
        s = ext[:, sl]
        step = 1
        while step < size:
            s = s + pltpu.roll(s, step, axis=0)
            step *= 2
        cnt = jnp.clip(pos + (pool_hist + 1), 1, size).astype(F32)
        pooled = s[POOL_MAX:, :] / cnt - xp[:, sl]
        yc.append(jnp.dot(pooled.astype(BF), pw_ref[gi], preferred_element_type=F32))
    yc_ref[...] = (jnp.concatenate(yc, axis=1) * ps_ref[...]).astype(yc_ref.dtype)
    exp_[0:POOL_MAX, :] = exp_[tt:tt + POOL_MAX, :]
    pst_ref[0] = exp_[0:POOL_MAX, :]


def _seq_mixer(x, lw, conv_hist8, h0, pool_hist16, *, nb, nt, tt, row0, pad, pool_hist):
    n = nb * nt * tt
    assert row0 % tt == 0
    xrow = lambda b, t: (row0 // tt + b * nt + t, 0)
    row = lambda b, t: (b * nt + t, 0)
    full2 = lambda b, t: (0, 0)
    full3 = lambda b, t: (0, 0, 0)
    per_b = lambda b, t: (b, 0, 0)
    kern = functools.partial(_seq_kernel, tt=tt, pad=pad, pool_hist=pool_hist)
    return pl.pallas_call(
        kern,
        grid=(nb, nt),
        in_specs=[
            pl.BlockSpec((tt, D_MODEL), xrow),
            pl.BlockSpec((1, D_MODEL), full2),
            pl.BlockSpec((D_MODEL, SEQ_COLS), full2),
            pl.BlockSpec((CONV_W, D_RNN), full2),
            pl.BlockSpec((1, D_RNN), full2),
            pl.BlockSpec((D_RNN // MXU_TILE, MXU_TILE, MXU_TILE), full3),
            pl.BlockSpec((D_RNN // MXU_TILE, MXU_TILE, MXU_TILE), full3),
            pl.BlockSpec((1, D_RNN), full2),
            pl.BlockSpec((1, D_RNN), full2),
            pl.BlockSpec((1, D_RNN), full2),
            pl.BlockSpec((1, D_KV), full2),
            pl.BlockSpec((MXU_TILE, MXU_TILE), full2),
            pl.BlockSpec((len(POOL_SIZES), POOL_GROUP, POOL_GROUP), full3),
            pl.BlockSpec((1, D_POOL), full2),
            pl.BlockSpec((1, SUBLANES, D_RNN), per_b),
            pl.BlockSpec((1, 1, D_RNN), per_b),
            pl.BlockSpec((1, POOL_MAX, D_POOL), per_b),
        ],
        out_specs=[
            pl.BlockSpec((tt, D_RNN), row),
            pl.BlockSpec((tt, D_POOL), row),
            pl.BlockSpec((tt, D_KV), row),
            pl.BlockSpec((tt, D_KV), row),
            pl.BlockSpec((1, SUBLANES, D_RNN), per_b),
            pl.BlockSpec((1, 1, D_RNN), per_b),
            pl.BlockSpec((1, POOL_MAX, D_POOL), per_b),
        ],
        out_shape=[
            jax.ShapeDtypeStruct((n, D_RNN), BF),
            jax.ShapeDtypeStruct((n, D_POOL), BF),
            jax.ShapeDtypeStruct((n, D_KV), F32),
            jax.ShapeDtypeStruct((n, D_KV), F32),
            jax.ShapeDtypeStruct((nb, SUBLANES, D_RNN), F32),
            jax.ShapeDtypeStruct((nb, 1, D_RNN), F32),
            jax.ShapeDtypeStruct((nb, POOL_MAX, D_POOL), F32),
        ],
        scratch_shapes=[
            pltpu.VMEM((tt + SUBLANES, D_RNN), F32),
            pltpu.VMEM((tt + POOL_MAX, D_POOL), F32),
            pltpu.VMEM((SUBLANES, D_RNN), F32),
        ],
        compiler_params=pltpu.CompilerParams(
            dimension_semantics=("arbitrary", "arbitrary"), vmem_limit_bytes=VMEM_LIMIT),
        name="seq_mixer",
    )(x, lw["norm_mix"], lw["w_seq"], lw["conv_w"], lw["conv_b"], lw["wa"], lw["wx"], lw["ba"], lw["bx"],
      lw["lam"], lw["k_norm"], lw["ones_bd"], lw["pool_w"], lw["pool_scale"], conv_hist8, h0, pool_hist16)


def _attend(q4, kcat, vcat, sink_col, col_valid):
    lg = lax.dot_general(q4, kcat, (((1,), (1,)), ((), ())), preferred_element_type=F32) * (HEAD_DIM ** -0.5)
    if col_valid is not None:
        lg = jnp.where(col_valid, lg, NEG_INF)
    m = jnp.maximum(jnp.max(lg, axis=-1, keepdims=True), sink_col)
    e = jnp.exp(lg - m)
    p = e / (jnp.sum(e, axis=-1, keepdims=True) + jnp.exp(sink_col - m))
    return jnp.dot(p.astype(BF), vcat, preferred_element_type=F32)


def _project_q(x_ref, g_ref, wq_ref, qn_ref, ones_ref):
    hb = _rmsnorm(x_ref[...], g_ref[...]).astype(BF)
    q = jnp.dot(hb, wq_ref[...], preferred_element_type=F32)
    return _head_rmsnorm(q, qn_ref[...], ones_ref[...]).astype(BF)


def _sink_column(sink_ref, kh, rows_per_head):
    r = lax.broadcasted_iota(jnp.int32, (GQA_GROUP * rows_per_head, 1), 0)
    col = jnp.full((GQA_GROUP * rows_per_head, 1), sink_ref[kh * GQA_GROUP], F32)
    for gq in range(1, GQA_GROUP):
        col = jnp.where(r >= gq * rows_per_head, sink_ref[kh * GQA_GROUP + gq], col)
    return col


def _attn_prompt_kernel(sink_ref, x_ref, g_ref, wq_ref, qn_ref, ones_ref,
                        kc_ref, kh1_ref, kh2_ref, km_ref, vc_ref, vh1_ref, vh2_ref, vm_ref,
                        yb_ref, *, cpt):
    g = pl.program_id(1)
    q = _project_q(x_ref, g_ref, wq_ref, qn_ref, ones_ref)
    kwin = jnp.concatenate([kh1_ref[...], kh2_ref[...], kc_ref[...]], axis=0).astype(BF)
    vwin = jnp.concatenate([vh1_ref[...], vh2_ref[...], vc_ref[...]], axis=0).astype(BF)
    kmeta = km_ref[PROMPT_PAD:CHUNK, :].astype(BF)
    vmeta = vm_ref[PROMPT_PAD:CHUNK, :].astype(BF)
    n_keys = N_META + 3 * CHUNK
    col = lax.broadcasted_iota(jnp.int32, (1, n_keys), 1)
    for j in range(cpt):
        c = g * cpt + j
        first = N_META + CHUNK * jnp.clip(3 - c, 0, 3)
        first = jnp.where(c >= 1, first, n_keys)
        valid = (col < N_META) | (col >= first)
        for kh in range(N_KV):
            hs = slice(kh * HEAD_DIM, (kh + 1) * HEAD_DIM)
            kcat = jnp.concatenate([kmeta[:, hs], kwin[j * CHUNK:(j + 3) * CHUNK, hs]], axis=0)
            vcat = jnp.concatenate([vmeta[:, hs], vwin[j * CHUNK:(j + 3) * CHUNK, hs]], axis=0)
            q4 = jnp.concatenate(
                [q[j * CHUNK:(j + 1) * CHUNK, (kh * GQA_GROUP + gq) * HEAD_DIM:(kh * GQA_GROUP + gq + 1) * HEAD_DIM]
                 for gq in range(GQA_GROUP)], axis=0)
            o = _attend(q4, kcat, vcat, _sink_column(sink_ref, kh, CHUNK), valid)
            for gq in range(GQA_GROUP):
                hq = kh * GQA_GROUP + gq
                yb_ref[j * CHUNK:(j + 1) * CHUNK, hq * HEAD_DIM:(hq + 1) * HEAD_DIM] = (
                    o[gq * CHUNK:(gq + 1) * CHUNK, :].astype(yb_ref.dtype))


def _attn_prompt(x, k, v, lw, *, nb, cpt):
    n = k.shape[0]
    nchunks = n // (nb * CHUNK)
    nt = nchunks // cpt
    tt = cpt * CHUNK
    row = lambda b, g: (b * nt + g, 0)
    full2 = lambda b, g: (0, 0)
    halo1 = lambda b, g: (b * nchunks + jnp.maximum(g * cpt - 2, 0), 0)
    halo2 = lambda b, g: (b * nchunks + jnp.maximum(g * cpt - 1, 0), 0)
    meta = lambda b, g: (b * nchunks, 0)
    kv_specs = [pl.BlockSpec((tt, D_KV), row), pl.BlockSpec((CHUNK, D_KV), halo1),
                pl.BlockSpec((CHUNK, D_KV), halo2), pl.BlockSpec((CHUNK, D_KV), meta)]
    return pl.pallas_call(
        functools.partial(_attn_prompt_kernel, cpt=cpt),
        grid=(nb, nt),
        in_specs=[
            pl.BlockSpec(memory_space=pltpu.SMEM),
            pl.BlockSpec((tt, D_MODEL), row),
            pl.BlockSpec((1, D_MODEL), full2),
            pl.BlockSpec((D_MODEL, D_MODEL), full2),
            pl.BlockSpec((1, D_MODEL), full2),
            pl.BlockSpec((MXU_TILE, MXU_TILE), full2),
        ] + kv_specs + kv_specs,
        out_specs=pl.BlockSpec((tt, D_MODEL), row),
        out_shape=jax.ShapeDtypeStruct((n, D_MODEL), BF),
        compiler_params=pltpu.CompilerParams(
            dimension_semantics=("arbitrary", "arbitrary"), vmem_limit_bytes=VMEM_LIMIT),
        name="attn_prompt",
    )(lw["sinks"], x, lw["norm_mix"], lw["w_q"], lw["q_norm"], lw["ones_bd"], k, k, k, k, v, v, v, v)


def _attn_sample_kernel(sink_ref, x_ref, g_ref, wq_ref, qn_ref, ones_ref,
                        kn_ref, kw_ref, km_ref, vn_ref, vw_ref, vm_ref, yb_ref, *, t_new):
    q = _project_q(x_ref, g_ref, wq_ref, qn_ref, ones_ref)
    kall = jnp.concatenate([km_ref[0], kw_ref[0], kn_ref[...]], axis=0).astype(BF)
    vall = jnp.concatenate([vm_ref[0], vw_ref[0], vn_ref[...]], axis=0).astype(BF)
    for kh in range(N_KV):
        hs = slice(kh * HEAD_DIM, (kh + 1) * HEAD_DIM)
        q4 = jnp.concatenate(
            [q[:, (kh * GQA_GROUP + gq) * HEAD_DIM:(kh * GQA_GROUP + gq + 1) * HEAD_DIM]
             for gq in range(GQA_GROUP)], axis=0)
        o = _attend(q4, kall[:, hs], vall[:, hs], _sink_column(sink_ref, kh, t_new), None)
        for gq in range(GQA_GROUP):
            hq = kh * GQA_GROUP + gq
            yb_ref[:, hq * HEAD_DIM:(hq + 1) * HEAD_DIM] = o[gq * t_new:(gq + 1) * t_new, :].astype(yb_ref.dtype)


def _attn_sample(x, k, v, meta_k, meta_v, win_k, win_v, lw, *, nb, t_new, row0):
    n = nb * t_new
    assert row0 % t_new == 0
    xrow = lambda b: (row0 // t_new + b, 0)
    row = lambda b: (b, 0)
    full2 = lambda b: (0, 0)
    per_b = lambda b: (b, 0, 0)
    kv_specs = [pl.BlockSpec((t_new, D_KV), row), pl.BlockSpec((1, WINDOW, D_KV), per_b),
                pl.BlockSpec((1, N_META, D_KV), per_b)]
    return pl.pallas_call(
        functools.partial(_attn_sample_kernel, t_new=t_new),
        grid=(nb,),
        in_specs=[
            pl.BlockSpec(memory_space=pltpu.SMEM),
            pl.BlockSpec((t_new, D_MODEL), xrow),
            pl.BlockSpec((1, D_MODEL), full2),
            pl.BlockSpec((D_MODEL, D_MODEL), full2),
            pl.BlockSpec((1, D_MODEL), full2),
            pl.BlockSpec((MXU_TILE, MXU_TILE), full2),
        ] + kv_specs + kv_specs,
        out_specs=pl.BlockSpec((t_new, D_MODEL), row),
        out_shape=jax.ShapeDtypeStruct((n, D_MODEL), BF),
        compiler_params=pltpu.CompilerParams(
            dimension_semantics=("arbitrary",), vmem_limit_bytes=VMEM_LIMIT),
        name="attn_sample",
    )(lw["sinks"], x, lw["norm_mix"], lw["w_q"], lw["q_norm"], lw["ones_bd"], k, win_k, meta_k, v, win_v, meta_v)


def _merge_kernel(x_ref, *refs):
    y_refs, (g_ref, wg_ref, gb_ref, wb_ref, wo_ref, o_ref) = refs[:3 * N_BRANCH], refs[3 * N_BRANCH:]
    last = pl.program_id(0) == pl.num_programs(0) - 1
    x = x_ref[...]
    hb = _rmsnorm(x, g_ref[...]).astype(BF)
    merged = None
    for nbr in range(N_BRANCH):
        lo_ref, hi_ref, s_ref = y_refs[3 * nbr:3 * nbr + 3]
        y = jnp.concatenate([lo_ref[...], jnp.where(last, s_ref[...], hi_ref[...])], axis=0)
        gl = jnp.dot(hb, wg_ref[:, nbr * D_MODEL:(nbr + 1) * D_MODEL], preferred_element_type=F32)
        gate = jax.nn.sigmoid(gl + gb_ref[:, nbr * D_MODEL:(nbr + 1) * D_MODEL])
        term = gate * jnp.dot(y, wb_ref[nbr], preferred_element_type=F32)
        merged = term if merged is None else merged + term
    o_ref[...] = x + jnp.dot(merged.astype(BF), wo_ref[...], preferred_element_type=F32)


def _merge(x, ys_prompt, ys_sample, lw, *, tm):
    n = x.shape[0]
    half = tm // 2
    n_p = ys_prompt[0].shape[0]
    assert n % tm == 0 and n_p % tm == half and ys_sample[0].shape[0] == half
    last_half = n_p // half - 1
    row = lambda i: (i, 0)
    lo = lambda i: (jnp.minimum(2 * i, last_half), 0)
    hi = lambda i: (jnp.minimum(2 * i + 1, last_half), 0)
    full2 = lambda i: (0, 0)
    full3 = lambda i: (0, 0, 0)
    tok = pl.BlockSpec((tm, D_MODEL), row)
    y_specs, y_args = [], []
    for yp, ys in zip(ys_prompt, ys_sample):
        y_specs += [pl.BlockSpec((half, D_MODEL), lo), pl.BlockSpec((half, D_MODEL), hi),
                    pl.BlockSpec((half, D_MODEL), full2)]
        y_args += [yp, yp, ys]
    return pl.pallas_call(
        _merge_kernel,
        grid=(n // tm,),
        in_specs=[tok] + y_specs + [
                  pl.BlockSpec((1, D_MODEL), full2),
                  pl.BlockSpec((D_MODEL, N_BRANCH * D_MODEL), full2),
                  pl.BlockSpec((1, N_BRANCH * D_MODEL), full2),
                  pl.BlockSpec((N_BRANCH, D_MODEL, D_MODEL), full3),
                  pl.BlockSpec((D_MODEL, D_MODEL), full2)],
        out_specs=tok,
        out_shape=jax.ShapeDtypeStruct((n, D_MODEL), F32),
        compiler_params=pltpu.CompilerParams(
            dimension_semantics=("arbitrary",), vmem_limit_bytes=VMEM_LIMIT),
        name="merge",
    )(x, *y_args, lw["norm_mix"], lw["w_g"], lw["gate_b"], lw["w_branch"], lw["w_out"])


def _top_values(s, k, with_rank=False):
    vals = []
    cur = s
    rank = jnp.full(s.shape, float(k), F32)
    for i in range(k):
        m = jnp.max(cur, axis=0, keepdims=True)
        vals.append(m)
        eq = cur == m
        if with_rank:
            rank = jnp.where(eq, float(i), rank)
        cur = jnp.where(eq, -jnp.inf, cur)
    top = jnp.concatenate(vals, axis=0)
    return (top, rank) if with_rank else top


def _kth_largest(s, k):
    cur = s
    left = jnp.full((1, s.shape[1]), float(k), F32)
    thr = jnp.full((1, s.shape[1]), jnp.inf, F32)
    for _ in range(k):
        m = jnp.max(cur, axis=0, keepdims=True)
        eq = cur == m
        thr = jnp.where(left > 0.0, m, thr)
        left = left - jnp.sum(eq.astype(F32), axis=0, keepdims=True)
        cur = jnp.where(eq, -jnp.inf, cur)
    return thr


def _pruned_candidates(t0, t1):
    k = t0.shape[0]
    return jnp.concatenate([t0[a:a + 1, :] + t1[0:k // (a + 1), :] for a in range(k)], axis=0)


def _peer_kernel(x_ref, g_ref, wqt_ref, keys_ref, u_ref, vt_ref, o_ref,
                 hb_s, qt_s, sc_s, r1_s, e1_s, cnt_s, a0_s, rows_s, hut_s, coef_s, acc_s, *, tm, ec):
    c = pl.program_id(1)
    nc = pl.num_programs(1)
    nblk = ec // PEER_KEYS
    ncol = tm // LANES

    @pl.when(c == 0)
    def _():
        hb = _rmsnorm(x_ref[...], g_ref[...]).astype(BF)
        hb_s[...] = hb
        qt_s[...] = lax.dot_general(wqt_ref[...], hb, (((1,), (1,)), ((), ())),
                                    preferred_element_type=F32).astype(BF)
        for h in range(PEER_HEADS):
            for p in range(2):
                r0 = (h * 2 + p) * PEER_KEYS
                sc_s[p] = jnp.dot(keys_ref[h * 2 + p], qt_s[r0:r0 + PEER_KEYS, :], preferred_element_type=F32)
            rs = slice(h * PEER_KEYS, (h + 1) * PEER_KEYS)
            for cb in range(ncol):
                ls = slice(cb * LANES, (cb + 1) * LANES)
                s0 = sc_s[0, :, ls]
                s1 = sc_s[1, :, ls]
                t0 = _top_values(s0, PEER_TOPK)
                t1, rank1 = _top_values(s1, PEER_TOPK, with_rank=True)
                cand = _pruned_candidates(t0, t1)
                thr = _kth_largest(cand, PEER_TOPK)
                m0 = t0[0:1, :]
                m1 = t1[0:1, :]
                zsum = jnp.sum(jnp.where(cand >= thr, jnp.exp(cand - (m0 + m1)), 0.0), axis=0, keepdims=True)
                cnt = jnp.zeros_like(s0)
                for b in range(PEER_TOPK):
                    keep = (t0 + t1[b:b + 1, :]) >= thr
                    sigma = jnp.min(jnp.where(keep, t0, jnp.inf), axis=0, keepdims=True)
                    cnt = cnt + (s0 >= sigma).astype(F32)
                r1_s[rs, ls] = rank1.astype(r1_s.dtype)
                e1_s[rs, ls] = (jnp.exp(s1 - m1) / zsum).astype(e1_s.dtype)
                cnt_s[rs, ls] = cnt
                a0_s[rs, ls] = jnp.exp(s0 - m0)
        acc_s[...] = jnp.zeros_like(acc_s)


    for h in range(PEER_HEADS):
        grp = pl.multiple_of(h * PEER_KEYS + c * nblk, SUBLANES)
        rows_s[0, h * nblk:(h + 1) * nblk, :] = cnt_s[pl.ds(grp, nblk), :]
        rows_s[1, h * nblk:(h + 1) * nblk, :] = a0_s[pl.ds(grp, nblk), :]

    sqrt_half = math.sqrt(0.5)
    gdt = r1_s.dtype
    hut_s[...] = lax.dot_general(u_ref[...], hb_s[...], (((1,), (1,)), ((), ())), preferred_element_type=F32)

    def build(pc):
        for il in range(pc * PEER_PIECE // PEER_KEYS, (pc + 1) * PEER_PIECE // PEER_KEYS):
            bs = slice(il * PEER_KEYS, (il + 1) * PEER_KEYS)
            for cb in range(ncol):
                ls = slice(cb * LANES, (cb + 1) * LANES)
                gate = jnp.zeros((PEER_KEYS, LANES), gdt)
                for h in range(PEER_HEADS):
                    rs = slice(h * PEER_KEYS, (h + 1) * PEER_KEYS)
                    cnt_row = rows_s[0, h * nblk + il:h * nblk + il + 1, ls].astype(gdt)
                    a0_row = rows_s[1, h * nblk + il:h * nblk + il + 1, ls].astype(gdt)
                    keep = jnp.minimum(jnp.maximum(cnt_row - r1_s[rs, ls], 0.0), 1.0)
                    gate = gate + (keep * e1_s[rs, ls]) * a0_row
                hu = hut_s[bs, ls]
                act = 0.5 * hu * (1.0 + lax.erf(hu * sqrt_half))
                coef_s[bs, ls] = (gate * act.astype(gdt)).astype(coef_s.dtype)

    def accumulate(pc):
        es = slice(pc * PEER_PIECE, (pc + 1) * PEER_PIECE)
        acc_s[...] += jnp.dot(vt_ref[:, es], coef_s[es, :], preferred_element_type=F32)

    npc = ec // PEER_PIECE
    build(0)
    for pc in range(1, npc):
        build(pc)
        accumulate(pc - 1)
    accumulate(npc - 1)

    @pl.when(c == nc - 1)
    def _():
        o_ref[...] = x_ref[...] + acc_s[...].T


def _peer(x, lw, *, tm, ec):
    n = x.shape[0]
    assert n % tm == 0 and tm % MXU_TILE == 0 and ec % (PEER_KEYS * SUBLANES) == 0 and N_EXPERTS % ec == 0
    tok = pl.BlockSpec((tm, D_MODEL), lambda i, c: (i, 0))
    nq = PEER_HEADS * 2 * PEER_KEYS
    nkeys = PEER_HEADS * PEER_KEYS
    return pl.pallas_call(
        functools.partial(_peer_kernel, tm=tm, ec=ec),
        grid=(n // tm, N_EXPERTS // ec),
        in_specs=[tok,
                  pl.BlockSpec((1, D_MODEL), lambda i, c: (0, 0)),
                  pl.BlockSpec((nq, D_MODEL), lambda i, c: (0, 0), pipeline_mode=pl.Buffered(1)),
                  pl.BlockSpec((PEER_HEADS * 2, PEER_KEYS, PEER_KEYS), lambda i, c: (0, 0, 0),
                               pipeline_mode=pl.Buffered(1)),
                  pl.BlockSpec((ec, D_MODEL), lambda i, c: (c, 0)),
                  pl.BlockSpec((D_MODEL, ec), lambda i, c: (0, c))],
        out_specs=tok,
        out_shape=jax.ShapeDtypeStruct((n, D_MODEL), F32),
        scratch_shapes=[
            pltpu.VMEM((tm, D_MODEL), BF),
            pltpu.VMEM((nq, tm), BF),
            pltpu.VMEM((2, PEER_KEYS, tm), F32),
            pltpu.VMEM((nkeys, tm), BF),
            pltpu.VMEM((nkeys, tm), BF),
            pltpu.VMEM((nkeys, tm), F32),
            pltpu.VMEM((nkeys, tm), F32),
            pltpu.VMEM((2, PEER_HEADS * (ec // PEER_KEYS), tm), F32),
            pltpu.VMEM((ec, tm), F32),
            pltpu.VMEM((ec, tm), BF),
            pltpu.VMEM((D_MODEL, tm), F32),
        ],
        compiler_params=pltpu.CompilerParams(
            dimension_semantics=("arbitrary", "arbitrary"), vmem_limit_bytes=VMEM_LIMIT),
        name="peer",
    )(x, lw["norm_ffn"], lw["wq_t"], lw["keys"], lw["u"], lw["v_t"])


def _block_diag_pack(w):
    per = MXU_TILE // RNN_BLOCK
    nsup = w.shape[0] // per
    out = jnp.zeros((nsup, MXU_TILE, MXU_TILE), w.dtype)
    for s in range(nsup):
        for j in range(per):
            out = out.at[s, j * RNN_BLOCK:(j + 1) * RNN_BLOCK, j * RNN_BLOCK:(j + 1) * RNN_BLOCK].set(w[s * per + j])
    return out


def _layer_weights(l, norm_mix, norm_ffn, w_in, conv_w, conv_b, lru_wa, lru_ba, lru_wx, lru_bx, lru_lambda,
                   q_norm, k_norm, attn_sinks, pool_w, pool_scale, gate_b, w_branch, w_out,
                   peer_wq, peer_keys, peer_u, peer_v):
    w = w_in[l]
    o_q = 2 * D_RNN
    o_k = o_q + D_MODEL
    o_v = o_k + D_KV
    o_p = o_v + D_KV
    o_g = o_p + D_POOL
    ones_bd = jnp.asarray(np.kron(np.eye(MXU_TILE // HEAD_DIM), np.ones((HEAD_DIM, HEAD_DIM))), BF)
    return {
        "norm_mix": norm_mix[l][None, :],
        "norm_ffn": norm_ffn[l][None, :],
        "w_seq": jnp.concatenate([w[:, :o_q], w[:, o_p:o_g], w[:, o_k:o_p]], axis=1).astype(BF),
        "w_q": w[:, o_q:o_k].astype(BF),
        "w_g": w[:, o_g:].astype(BF),
        "conv_w": conv_w[l],
        "conv_b": conv_b[l][None, :],
        "wa": _block_diag_pack(lru_wa[l]).astype(BF),
        "wx": _block_diag_pack(lru_wx[l]).astype(BF),
        "ba": lru_ba[l][None, :],
        "bx": lru_bx[l][None, :],
        "lam": lru_lambda[l][None, :],
        "q_norm": jnp.tile(q_norm[l], N_HEADS)[None, :],
        "k_norm": jnp.tile(k_norm[l], N_KV)[None, :],
        "ones_bd": ones_bd,
        "sinks": attn_sinks[l],
        "pool_w": pool_w[l].astype(BF),
        "pool_scale": pool_scale[l][None, :],
        "gate_b": gate_b[l].reshape(1, N_BRANCH * D_MODEL),
        "w_branch": w_branch[l].astype(BF),
        "w_out": w_out[l].astype(BF),
        "wq_t": peer_wq[l].T.astype(BF),
        "keys": peer_keys[l].reshape(PEER_HEADS * 2, PEER_KEYS, PEER_KEYS).astype(BF),
        "u": peer_u[l].astype(BF),
        "v_t": peer_v[l].T.astype(BF),
    }


def kernel(x_prompt, x_sample, state_rglru_h, state_rglru_conv, cache_meta_k, cache_meta_v, cache_win_k, cache_win_v, state_pool, meta_tokens, norm_mix, norm_ffn, w_in, conv_w, conv_b, lru_wa, lru_ba, lru_wx, lru_bx, lru_lambda, q_norm, k_norm, attn_sinks, pool_w, pool_scale, gate_b, w_branch, w_out, peer_wq, peer_keys, peer_u, peer_v):
    bp, seq, _ = x_prompt.shape
    bs, ts, _ = x_sample.shape
    depth = w_in.shape[0]
    dt = x_prompt.dtype
    s_pad = PROMPT_PAD + N_META + seq
    nchunks = s_pad // CHUNK
    cpt = 3 if nchunks % 3 == 0 else 1
    tt_p = cpt * CHUNK

    xp = jnp.concatenate([jnp.zeros((bp, PROMPT_PAD, D_MODEL), dt),
                          jnp.broadcast_to(meta_tokens.astype(dt)[None], (bp, N_META, D_MODEL)), x_prompt], axis=1)
    n_p = bp * s_pad
    n_s = bs * ts
    x = jnp.concatenate([xp.reshape(n_p, D_MODEL), x_sample.reshape(n_s, D_MODEL)], axis=0)
    tm = 2 * n_s
    ec = 2048

    zeros_ch = jnp.zeros((bp, SUBLANES, D_RNN), dt)
    zeros_h = jnp.zeros((bp, 1, D_RNN), dt)
    zeros_ph = jnp.zeros((bp, POOL_MAX, D_POOL), dt)

    outs = {k: [] for k in ("ph", "pc", "pmk", "pmv", "pwk", "pwv", "ppool", "sh", "sc", "sk", "sv", "spool")}
    for l in range(depth):
        lw = _layer_weights(l, norm_mix, norm_ffn, w_in, conv_w, conv_b, lru_wa, lru_ba, lru_wx, lru_bx,
                            lru_lambda, q_norm, k_norm, attn_sinks, pool_w, pool_scale, gate_b, w_branch,
                            w_out, peer_wq, peer_keys, peer_u, peer_v)
        ya, yc, k, v, cs, hl, pst = _seq_mixer(x, lw, zeros_ch, zeros_h, zeros_ph, nb=bp, nt=s_pad // tt_p,
                                               tt=tt_p, row0=0, pad=PROMPT_PAD, pool_hist=0)
        yb = _attn_prompt(x, k, v, lw, nb=bp, cpt=cpt)
        ys_prompt = (ya, yb, yc)
        k4 = k.reshape(bp, s_pad, N_KV, HEAD_DIM)
        v4 = v.reshape(bp, s_pad, N_KV, HEAD_DIM)
        outs["ph"].append(hl[:, 0])
        outs["pc"].append(cs[:, SUBLANES - (CONV_W - 1):])
        outs["pmk"].append(k4[:, PROMPT_PAD:PROMPT_PAD + N_META])
        outs["pmv"].append(v4[:, PROMPT_PAD:PROMPT_PAD + N_META])
        outs["pwk"].append(k4[:, -WINDOW:])
        outs["pwv"].append(v4[:, -WINDOW:])
        outs["ppool"].append(pst[:, 1:])
        ch8 = jnp.pad(state_rglru_conv[l], ((0, 0), (SUBLANES - (CONV_W - 1), 0), (0, 0)))
        ph16 = jnp.pad(state_pool[l], ((0, 0), (1, 0), (0, 0)))
        ya, yc, k, v, cs, hl, pst = _seq_mixer(x, lw, ch8, state_rglru_h[l][:, None, :], ph16, nb=bs, nt=1,
                                               tt=ts, row0=n_p, pad=0, pool_hist=POOL_MAX - 1)
        yb = _attn_sample(x, k, v, cache_meta_k[l].reshape(bs, N_META, D_KV),
                          cache_meta_v[l].reshape(bs, N_META, D_KV),
                          cache_win_k[l].reshape(bs, WINDOW, D_KV), cache_win_v[l].reshape(bs, WINDOW, D_KV),
                          lw, nb=bs, t_new=ts, row0=n_p)
        x = _merge(x, ys_prompt, (ya, yb, yc), lw, tm=tm)
        x = _peer(x, lw, tm=tm, ec=ec)
        outs["sh"].append(hl[:, 0])
        outs["sc"].append(cs[:, SUBLANES - (CONV_W - 1):])
        outs["sk"].append(k.reshape(bs, ts, N_KV, HEAD_DIM))
        outs["sv"].append(v.reshape(bs, ts, N_KV, HEAD_DIM))
        outs["spool"].append(pst[:, 1:])

    y_prompt = x[:n_p].reshape(bp, s_pad, D_MODEL)[:, PROMPT_PAD + N_META:]
    y_sample = x[n_p:].reshape(bs, ts, D_MODEL)
    st = lambda key: jnp.stack(outs[key])
    return (y_prompt, y_sample, st("ph"), st("pc"), st("pmk"), st("pmv"), st("pwk"), st("pwv"), st("ppool"),
            st("sh"), st("sc"), st("sk"), st("sv"), st("spool"))
```

```python
import functools
import math

import jax
import jax.numpy as jnp
import numpy as np
from jax import lax
from jax.experimental import pallas as pl
from jax.experimental.pallas import tpu as pltpu

D_MODEL = 1024
CHUNK = 64
N_META = 16
EPS = 1e-6
D_RNN = 1024
RNN_BLOCK = 64
CONV_W = 4
LRU_C = 8.0
HEAD_DIM = 64
N_HEADS = 16
N_KV = 4
GQA_GROUP = N_HEADS // N_KV
D_KV = N_KV * HEAD_DIM
WINDOW = 128
D_POOL = 1024
POOL_SIZES = (2, 4, 8, 16)
POOL_GROUP = 256
POOL_MAX = 16
N_BRANCH = 3
PEER_HEADS = 8
PEER_KEYS = 128
N_EXPERTS = PEER_KEYS * PEER_KEYS
PEER_TOPK = 16
PEER_PIECE = 512
PEER_LEAD = 2
PEER_LAG = 2
NEG_INF = -1e30

LANES = 128
SUBLANES = 8
MXU_TILE = 256
VMEM_LIMIT = 56 * 1024 * 1024

PROMPT_PAD = CHUNK - N_META
SEQ_COLS = 2 * D_RNN + D_POOL + 2 * D_KV

BF = jnp.bfloat16
F32 = jnp.float32


def _rmsnorm(x, g):
    xf = x.astype(F32)
    return xf * lax.rsqrt(jnp.mean(xf * xf, axis=-1, keepdims=True) + EPS) * g


def _head_sumsq(x, ones_bd):
    sq = x * x
    hi = sq.astype(BF)
    lo = (sq - hi.astype(F32)).astype(BF)
    out = []
    for c in range(x.shape[1] // MXU_TILE):
        sl = slice(c * MXU_TILE, (c + 1) * MXU_TILE)
        out.append(jnp.dot(hi[:, sl], ones_bd, preferred_element_type=F32)
                   + jnp.dot(lo[:, sl], ones_bd, preferred_element_type=F32))
    return out[0] if len(out) == 1 else jnp.concatenate(out, axis=1)


def _head_rmsnorm(x, g, ones_bd):
    ms = _head_sumsq(x, ones_bd) * (1.0 / HEAD_DIM)
    return x * lax.rsqrt(ms + EPS) * g


def _shift_rows(x, d, fill, rows):
    return jnp.where(rows < d, fill, pltpu.roll(x, d, axis=0))


def _seq_kernel(x_ref, g_ref, w_ref, cw_ref, cb_ref, wa_ref, wx_ref, ba_ref, bx_ref, lam_ref,
                kn_ref, ones_ref, pw_ref, ps_ref, ch_ref, h0_ref, ph_ref,
                ya_ref, yc_ref, k_ref, v_ref, cs_ref, hl_ref, pst_ref,
                exr, exp_, hc, *, tt, pad, pool_hist):
    t = pl.program_id(1)

    @pl.when(t == 0)
    def _():
        exr[0:SUBLANES, :] = ch_ref[0]
        exp_[0:POOL_MAX, :] = ph_ref[0]
        hc[...] = jnp.broadcast_to(h0_ref[0], hc.shape)

    hb = _rmsnorm(x_ref[...], g_ref[...]).astype(BF)
    z = jnp.dot(hb, w_ref[...], preferred_element_type=F32)
    xr = z[:, 0:D_RNN]
    gr = z[:, D_RNN:2 * D_RNN]
    xp = z[:, 2 * D_RNN:2 * D_RNN + D_POOL]
    kk = z[:, 2 * D_RNN + D_POOL:2 * D_RNN + D_POOL + D_KV]
    vv = z[:, 2 * D_RNN + D_POOL + D_KV:]

    rows = lax.broadcasted_iota(jnp.int32, (tt, 1), 0)
    pos = rows + (t * tt - pad)

    exr[SUBLANES:SUBLANES + tt, :] = xr
    xc = cb_ref[...] + cw_ref[3:4, :] * xr
    for j in range(CONV_W - 1):
        xc = xc + cw_ref[j:j + 1, :] * exr[SUBLANES - (CONV_W - 1) + j:SUBLANES - (CONV_W - 1) + j + tt, :]
    exr[0:SUBLANES, :] = exr[tt:tt + SUBLANES, :]
    cs_ref[0] = exr[0:SUBLANES, :]

    ra, ia = [], []
    for c in range(D_RNN // MXU_TILE):
        xcb = xc[:, c * MXU_TILE:(c + 1) * MXU_TILE].astype(BF)
        ra.append(jnp.dot(xcb, wa_ref[c], preferred_element_type=F32))
        ia.append(jnp.dot(xcb, wx_ref[c], preferred_element_type=F32))
    r = jax.nn.sigmoid(jnp.concatenate(ra, axis=1) + ba_ref[...])
    i = jax.nn.sigmoid(jnp.concatenate(ia, axis=1) + bx_ref[...])
    nl = -lam_ref[...]
    softplus = jnp.maximum(nl, 0.0) + jnp.log1p(jnp.exp(-jnp.abs(nl)))
    log_a = (-LRU_C * softplus) * r
    a = jnp.exp(log_a)
    u = jnp.sqrt(1.0 - jnp.exp(2.0 * log_a)) * (i * xc)
    if pad:
        u = jnp.where(pos >= 0, u, 0.0)

    sub = rows % SUBLANES
    d = 1
    while d < SUBLANES:
        a_s = _shift_rows(a, d, 1.0, sub)
        u_s = _shift_rows(u, d, 0.0, sub)
        u = a * u_s + u
        a = a * a_s
        d *= 2
    carry = hc[0:1, :]
    groups = []
    for gi in range(tt // SUBLANES):
        hg = a[gi * SUBLANES:(gi + 1) * SUBLANES, :] * carry + u[gi * SUBLANES:(gi + 1) * SUBLANES, :]
        groups.append(hg)
        carry = hg[SUBLANES - 1:SUBLANES, :]
    h = jnp.concatenate(groups, axis=0)
    hc[...] = jnp.broadcast_to(h[tt - 1:tt, :], hc.shape)
    hl_ref[0] = h[tt - 1:tt, :]
    ya_ref[...] = (h * jax.nn.gelu(gr, approximate=True)).astype(ya_ref.dtype)

    k_ref[...] = _head_rmsnorm(kk, kn_ref[...], ones_ref[...])
    v_ref[...] = vv

    exp_[POOL_MAX:POOL_MAX + tt, :] = xp
    ext = exp_[...]
    yc = []
    for gi, size in enumerate(POOL_SIZES):
        sl = slice(gi * POOL_GROUP, (gi + 1) * POOL_GROUP)
        s = ext[:, sl]
        step = 1
        while step < size:
            s = s + pltpu.roll(s, step, axis=0)
            step *= 2
        cnt = jnp.clip(pos + (pool_hist + 1), 1, size).astype(F32)
        pooled = s[POOL_MAX:, :] / cnt - xp[:, sl]
        yc.append(jnp.dot(pooled.astype(BF), pw_ref[gi], preferred_element_type=F32))
    yc_ref[...] = (jnp.concatenate(yc, axis=1) * ps_ref[...]).astype(yc_ref.dtype)
    exp_[0:POOL_MAX, :] = exp_[tt:tt + POOL_MAX, :]
    pst_ref[0] = exp_[0:POOL_MAX, :]


def _seq_mixer(x, lw, conv_hist8, h0, pool_hist16, *, nb, nt, tt, row0, pad, pool_hist):
    n = nb * nt * tt
    assert row0 % tt == 0
    xrow = lambda b, t: (row0 // tt + b * nt + t, 0)
    row = lambda b, t: (b * nt + t, 0)
    full2 = lambda b, t: (0, 0)
    full3 = lambda b, t: (0, 0, 0)
    per_b = lambda b, t: (b, 0, 0)
    kern = functools.partial(_seq_kernel, tt=tt, pad=pad, pool_hist=pool_hist)
    return pl.pallas_call(
        kern,
        grid=(nb, nt),
        in_specs=[
            pl.BlockSpec((tt, D_MODEL), xrow),
            pl.BlockSpec((1, D_MODEL), full2),
            pl.BlockSpec((D_MODEL, SEQ_COLS), full2),
            pl.BlockSpec((CONV_W, D_RNN), full2),
            pl.BlockSpec((1, D_RNN), full2),
            pl.BlockSpec((D_RNN // MXU_TILE, MXU_TILE, MXU_TILE), full3),
            pl.BlockSpec((D_RNN // MXU_TILE, MXU_TILE, MXU_TILE), full3),
            pl.BlockSpec((1, D_RNN), full2),
            pl.BlockSpec((1, D_RNN), full2),
            pl.BlockSpec((1, D_RNN), full2),
            pl.BlockSpec((1, D_KV), full2),
            pl.BlockSpec((MXU_TILE, MXU_TILE), full2),
            pl.BlockSpec((len(POOL_SIZES), POOL_GROUP, POOL_GROUP), full3),
            pl.BlockSpec((1, D_POOL), full2),
            pl.BlockSpec((1, SUBLANES, D_RNN), per_b),
            pl.BlockSpec((1, 1, D_RNN), per_b),
            pl.BlockSpec((1, POOL_MAX, D_POOL), per_b),
        ],
        out_specs=[
            pl.BlockSpec((tt, D_RNN), row),
            pl.BlockSpec((tt, D_POOL), row),
            pl.BlockSpec((tt, D_KV), row),
            pl.BlockSpec((tt, D_KV), row),
            pl.BlockSpec((1, SUBLANES, D_RNN), per_b),
            pl.BlockSpec((1, 1, D_RNN), per_b),
            pl.BlockSpec((1, POOL_MAX, D_POOL), per_b),
        ],
        out_shape=[
            jax.ShapeDtypeStruct((n, D_RNN), BF),
            jax.ShapeDtypeStruct((n, D_POOL), BF),
            jax.ShapeDtypeStruct((n, D_KV), F32),
            jax.ShapeDtypeStruct((n, D_KV), F32),
            jax.ShapeDtypeStruct((nb, SUBLANES, D_RNN), F32),
            jax.ShapeDtypeStruct((nb, 1, D_RNN), F32),
            jax.ShapeDtypeStruct((nb, POOL_MAX, D_POOL), F32),
        ],
        scratch_shapes=[
            pltpu.VMEM((tt + SUBLANES, D_RNN), F32),
            pltpu.VMEM((tt + POOL_MAX, D_POOL), F32),
            pltpu.VMEM((SUBLANES, D_RNN), F32),
        ],
        compiler_params=pltpu.CompilerParams(
            dimension_semantics=("arbitrary", "arbitrary"), vmem_limit_bytes=VMEM_LIMIT),
        name="seq_mixer",
    )(x, lw["norm_mix"], lw["w_seq"], lw["conv_w"], lw["conv_b"], lw["wa"], lw["wx"], lw["ba"], lw["bx"],
      lw["lam"], lw["k_norm"], lw["ones_bd"], lw["pool_w"], lw["pool_scale"], conv_hist8, h0, pool_hist16)


def _attend(q4, kcat, vcat, sink_col, col_valid):
    lg = lax.dot_general(q4, kcat, (((1,), (1,)), ((), ())), preferred_element_type=F32) * (HEAD_DIM ** -0.5)
    if col_valid is not None:
        lg = jnp.where(col_valid, lg, NEG_INF)
    m = jnp.maximum(jnp.max(lg, axis=-1, keepdims=True), sink_col)
    e = jnp.exp(lg - m)
    p = e / (jnp.sum(e, axis=-1, keepdims=True) + jnp.exp(sink_col - m))
    return jnp.dot(p.astype(BF), vcat, preferred_element_type=F32)


def _project_q(x_ref, g_ref, wq_ref, qn_ref, ones_ref):
    hb = _rmsnorm(x_ref[...], g_ref[...]).astype(BF)
    q = jnp.dot(hb, wq_ref[...], preferred_element_type=F32)
    return _head_rmsnorm(q, qn_ref[...], ones_ref[...]).astype(BF)


def _sink_column(sink_ref, kh, rows_per_head):
    r = lax.broadcasted_iota(jnp.int32, (GQA_GROUP * rows_per_head, 1), 0)
    col = jnp.full((GQA_GROUP * rows_per_head, 1), sink_ref[kh * GQA_GROUP], F32)
    for gq in range(1, GQA_GROUP):
        col = jnp.where(r >= gq * rows_per_head, sink_ref[kh * GQA_GROUP + gq], col)
    return col


def _attn_prompt_kernel(sink_ref, x_ref, g_ref, wq_ref, qn_ref, ones_ref,
                        kc_ref, kh1_ref, kh2_ref, km_ref, vc_ref, vh1_ref, vh2_ref, vm_ref,
                        yb_ref, *, cpt):
    g = pl.program_id(1)
    q = _project_q(x_ref, g_ref, wq_ref, qn_ref, ones_ref)
    kwin = jnp.concatenate([kh1_ref[...], kh2_ref[...], kc_ref[...]], axis=0).astype(BF)
    vwin = jnp.concatenate([vh1_ref[...], vh2_ref[...], vc_ref[...]], axis=0).astype(BF)
    kmeta = km_ref[PROMPT_PAD:CHUNK, :].astype(BF)
    vmeta = vm_ref[PROMPT_PAD:CHUNK, :].astype(BF)
    n_keys = N_META + 3 * CHUNK
    col = lax.broadcasted_iota(jnp.int32, (1, n_keys), 1)
    for j in range(cpt):
        c = g * cpt + j
        first = N_META + CHUNK * jnp.clip(3 - c, 0, 3)
        first = jnp.where(c >= 1, first, n_keys)
        valid = (col < N_META) | (col >= first)
        for kh in range(N_KV):
            hs = slice(kh * HEAD_DIM, (kh + 1) * HEAD_DIM)
            kcat = jnp.concatenate([kmeta[:, hs], kwin[j * CHUNK:(j + 3) * CHUNK, hs]], axis=0)
            vcat = jnp.concatenate([vmeta[:, hs], vwin[j * CHUNK:(j + 3) * CHUNK, hs]], axis=0)
            q4 = jnp.concatenate(
                [q[j * CHUNK:(j + 1) * CHUNK, (kh * GQA_GROUP + gq) * HEAD_DIM:(kh * GQA_GROUP + gq + 1) * HEAD_DIM]
                 for gq in range(GQA_GROUP)], axis=0)
            o = _attend(q4, kcat, vcat, _sink_column(sink_ref, kh, CHUNK), valid)
            for gq in range(GQA_GROUP):
                hq = kh * GQA_GROUP + gq
                yb_ref[j * CHUNK:(j + 1) * CHUNK, hq * HEAD_DIM:(hq + 1) * HEAD_DIM] = (
                    o[gq * CHUNK:(gq + 1) * CHUNK, :].astype(yb_ref.dtype))


def _attn_prompt(x, k, v, lw, *, nb, cpt):
    n = k.shape[0]
    nchunks = n // (nb * CHUNK)
    nt = nchunks // cpt
    tt = cpt * CHUNK
    row = lambda b, g: (b * nt + g, 0)
    full2 = lambda b, g: (0, 0)
    halo1 = lambda b, g: (b * nchunks + jnp.maximum(g * cpt - 2, 0), 0)
    halo2 = lambda b, g: (b * nchunks + jnp.maximum(g * cpt - 1, 0), 0)
    meta = lambda b, g: (b * nchunks, 0)
    kv_specs = [pl.BlockSpec((tt, D_KV), row), pl.BlockSpec((CHUNK, D_KV), halo1),
                pl.BlockSpec((CHUNK, D_KV), halo2), pl.BlockSpec((CHUNK, D_KV), meta)]
    return pl.pallas_call(
        functools.partial(_attn_prompt_kernel, cpt=cpt),
        grid=(nb, nt),
        in_specs=[
            pl.BlockSpec(memory_space=pltpu.SMEM),
            pl.BlockSpec((tt, D_MODEL), row),
            pl.BlockSpec((1, D_MODEL), full2),
            pl.BlockSpec((D_MODEL, D_MODEL), full2),
            pl.BlockSpec((1, D_MODEL), full2),
            pl.BlockSpec((MXU_TILE, MXU_TILE), full2),
        ] + kv_specs + kv_specs,
        out_specs=pl.BlockSpec((tt, D_MODEL), row),
        out_shape=jax.ShapeDtypeStruct((n, D_MODEL), BF),
        compiler_params=pltpu.CompilerParams(
            dimension_semantics=("arbitrary", "arbitrary"), vmem_limit_bytes=VMEM_LIMIT),
        name="attn_prompt",
    )(lw["sinks"], x, lw["norm_mix"], lw["w_q"], lw["q_norm"], lw["ones_bd"], k, k, k, k, v, v, v, v)


def _attn_sample_kernel(sink_ref, x_ref, g_ref, wq_ref, qn_ref, ones_ref,
                        kn_ref, kw_ref, km_ref, vn_ref, vw_ref, vm_ref, yb_ref, *, t_new):
    q = _project_q(x_ref, g_ref, wq_ref, qn_ref, ones_ref)
    kall = jnp.concatenate([km_ref[0], kw_ref[0], kn_ref[...]], axis=0).astype(BF)
    vall = jnp.concatenate([vm_ref[0], vw_ref[0], vn_ref[...]], axis=0).astype(BF)
    for kh in range(N_KV):
        hs = slice(kh * HEAD_DIM, (kh + 1) * HEAD_DIM)
        q4 = jnp.concatenate(
            [q[:, (kh * GQA_GROUP + gq) * HEAD_DIM:(kh * GQA_GROUP + gq + 1) * HEAD_DIM]
             for gq in range(GQA_GROUP)], axis=0)
        o = _attend(q4, kall[:, hs], vall[:, hs], _sink_column(sink_ref, kh, t_new), None)
        for gq in range(GQA_GROUP):
            hq = kh * GQA_GROUP + gq
            yb_ref[:, hq * HEAD_DIM:(hq + 1) * HEAD_DIM] = o[gq * t_new:(gq + 1) * t_new, :].astype(yb_ref.dtype)


def _attn_sample(x, k, v, meta_k, meta_v, win_k, win_v, lw, *, nb, t_new, row0):
    n = nb * t_new
    assert row0 % t_new == 0
    xrow = lambda b: (row0 // t_new + b, 0)
    row = lambda b: (b, 0)
    full2 = lambda b: (0, 0)
    per_b = lambda b: (b, 0, 0)
    kv_specs = [pl.BlockSpec((t_new, D_KV), row), pl.BlockSpec((1, WINDOW, D_KV), per_b),
                pl.BlockSpec((1, N_META, D_KV), per_b)]
    return pl.pallas_call(
        functools.partial(_attn_sample_kernel, t_new=t_new),
        grid=(nb,),
        in_specs=[
            pl.BlockSpec(memory_space=pltpu.SMEM),
            pl.BlockSpec((t_new, D_MODEL), xrow),
            pl.BlockSpec((1, D_MODEL), full2),
            pl.BlockSpec((D_MODEL, D_MODEL), full2),
            pl.BlockSpec((1, D_MODEL), full2),
            pl.BlockSpec((MXU_TILE, MXU_TILE), full2),
        ] + kv_specs + kv_specs,
        out_specs=pl.BlockSpec((t_new, D_MODEL), row),
        out_shape=jax.ShapeDtypeStruct((n, D_MODEL), BF),
        compiler_params=pltpu.CompilerParams(
            dimension_semantics=("arbitrary",), vmem_limit_bytes=VMEM_LIMIT),
        name="attn_sample",
    )(lw["sinks"], x, lw["norm_mix"], lw["w_q"], lw["q_norm"], lw["ones_bd"], k, win_k, meta_k, v, win_v, meta_v)


def _merge_kernel(x_ref, *refs):
    y_refs, (g_ref, wg_ref, gb_ref, wb_ref, wo_ref, o_ref) = refs[:3 * N_BRANCH], refs[3 * N_BRANCH:]
    last = pl.program_id(0) == pl.num_programs(0) - 1
    x = x_ref[...]
    hb = _rmsnorm(x, g_ref[...]).astype(BF)
    merged = None
    for nbr in range(N_BRANCH):
        lo_ref, hi_ref, s_ref = y_refs[3 * nbr:3 * nbr + 3]
        y = jnp.concatenate([lo_ref[...], jnp.where(last, s_ref[...], hi_ref[...])], axis=0)
        gl = jnp.dot(hb, wg_ref[:, nbr * D_MODEL:(nbr + 1) * D_MODEL], preferred_element_type=F32)
        gate = jax.nn.sigmoid(gl + gb_ref[:, nbr * D_MODEL:(nbr + 1) * D_MODEL])
        term = gate * jnp.dot(y, wb_ref[nbr], preferred_element_type=F32)
        merged = term if merged is None else merged + term
    o_ref[...] = x + jnp.dot(merged.astype(BF), wo_ref[...], preferred_element_type=F32)


def _merge(x, ys_prompt, ys_sample, lw, *, tm):
    n = x.shape[0]
    half = tm // 2
    n_p = ys_prompt[0].shape[0]
    assert n % tm == 0 and n_p % tm == half and ys_sample[0].shape[0] == half
    last_half = n_p // half - 1
    row = lambda i: (i, 0)
    lo = lambda i: (jnp.minimum(2 * i, last_half), 0)
    hi = lambda i: (jnp.minimum(2 * i + 1, last_half), 0)
    full2 = lambda i: (0, 0)
    full3 = lambda i: (0, 0, 0)
    tok = pl.BlockSpec((tm, D_MODEL), row)
    y_specs, y_args = [], []
    for yp, ys in zip(ys_prompt, ys_sample):
        y_specs += [pl.BlockSpec((half, D_MODEL), lo), pl.BlockSpec((half, D_MODEL), hi),
                    pl.BlockSpec((half, D_MODEL), full2)]
        y_args += [yp, yp, ys]
    return pl.pallas_call(
        _merge_kernel,
        grid=(n // tm,),
        in_specs=[tok] + y_specs + [
                  pl.BlockSpec((1, D_MODEL), full2),
                  pl.BlockSpec((D_MODEL, N_BRANCH * D_MODEL), full2),
                  pl.BlockSpec((1, N_BRANCH * D_MODEL), full2),
                  pl.BlockSpec((N_BRANCH, D_MODEL, D_MODEL), full3),
                  pl.BlockSpec((D_MODEL, D_MODEL), full2)],
        out_specs=tok,
        out_shape=jax.ShapeDtypeStruct((n, D_MODEL), F32),
        compiler_params=pltpu.CompilerParams(
            dimension_semantics=("arbitrary",), vmem_limit_bytes=VMEM_LIMIT),
        name="merge",
    )(x, *y_args, lw["norm_mix"], lw["w_g"], lw["gate_b"], lw["w_branch"], lw["w_out"])


def _top_values_by_value(s, k):
    vals = []
    cur = s
    rank = jnp.full(s.shape, float(k), F32)
    for i in range(k):
        m = jnp.max(cur, axis=0, keepdims=True)
        vals.append(m)
        eq = cur == m
        rank = jnp.where(eq, float(i), rank)
        cur = jnp.where(eq, -jnp.inf, cur)
    extra = jnp.sum((cur == -jnp.inf).astype(F32), axis=0, keepdims=True) - float(k)
    return jnp.concatenate(vals, axis=0), rank, extra


def _top_values(s, k):
    rows = lax.broadcasted_iota(jnp.int32, s.shape, 0).astype(F32)
    vals = []
    cur = s
    rank = jnp.full(s.shape, float(k), F32)
    for i in range(k):
        m = jnp.max(cur, axis=0, keepdims=True)
        vals.append(m)
        first = jnp.min(jnp.where(cur == m, rows, float(s.shape[0])), axis=0, keepdims=True)
        hit = rows == first
        rank = jnp.where(hit, float(i), rank)
        cur = jnp.where(hit, -jnp.inf, cur)
    return jnp.concatenate(vals, axis=0), rank


def _kth_largest(s, k):
    cur = s
    left = jnp.full((1, s.shape[1]), float(k), F32)
    thr = jnp.full((1, s.shape[1]), jnp.inf, F32)
    for _ in range(k):
        m = jnp.max(cur, axis=0, keepdims=True)
        eq = cur == m
        thr = jnp.where(left > 0.0, m, thr)
        left = left - jnp.sum(eq.astype(F32), axis=0, keepdims=True)
        cur = jnp.where(eq, -jnp.inf, cur)
    return thr


def _pruned_candidates(t0, t1):
    k = t0.shape[0]
    return jnp.concatenate([t0[a:a + 1, :] + t1[0:k // (a + 1), :] for a in range(k)], axis=0)


def _rank_by_value(s0, s1):
    t0, _, extra0 = _top_values_by_value(s0, PEER_TOPK)
    t1, rank1, extra1 = _top_values_by_value(s1, PEER_TOPK)
    cand = _pruned_candidates(t0, t1)
    thr = _kth_largest(cand, PEER_TOPK)
    m0 = t0[0:1, :]
    m1 = t1[0:1, :]
    chosen = cand >= thr
    zsum = jnp.sum(jnp.where(chosen, jnp.exp(cand - (m0 + m1)), 0.0), axis=0, keepdims=True)
    ties = extra0 + extra1 + (jnp.sum(chosen.astype(F32), axis=0, keepdims=True) - float(PEER_TOPK))
    cnt = jnp.zeros_like(s0)
    for b in range(PEER_TOPK):
        keep = (t0 + t1[b:b + 1, :]) >= thr
        sigma = jnp.min(jnp.where(keep, t0, jnp.inf), axis=0, keepdims=True)
        cnt = cnt + (s0 >= sigma).astype(F32)
    return rank1, jnp.exp(s1 - m1) / zsum, cnt, jnp.exp(s0 - m0), ties


def _rank_stable(s0, s1):
    t0, rank0 = _top_values(s0, PEER_TOPK)
    t1, rank1 = _top_values(s1, PEER_TOPK)
    cand = _pruned_candidates(t0, t1)
    thr = _kth_largest(cand, PEER_TOPK)
    m0 = t0[0:1, :]
    m1 = t1[0:1, :]
    above, equal = [], []
    for a in range(PEER_TOPK):
        row = t0[a:a + 1, :] + t1
        above.append(jnp.sum((row > thr).astype(F32), axis=0, keepdims=True))
        equal.append(jnp.sum((row == thr).astype(F32), axis=0, keepdims=True))
    need = float(PEER_TOPK) - functools.reduce(lambda p, q: p + q, above)
    e_thr = jnp.exp(thr - (m0 + m1))
    cnt = jnp.zeros_like(s0)
    zsum = jnp.zeros_like(thr)
    for a in range(PEER_TOPK):
        take = jnp.clip(need, 0.0, equal[a])
        need = need - equal[a]
        cnt = cnt + jnp.where(rank0 == float(a), above[a] + take, 0.0)
        row = t0[a:a + 1, :] + t1
        zsum = zsum + jnp.sum(jnp.where(row > thr, jnp.exp(row - (m0 + m1)), 0.0), axis=0,
                              keepdims=True) + take * e_thr
    return rank1, jnp.exp(s1 - m1) / zsum, cnt, jnp.exp(s0 - m0)


def _peer_kernel(x_ref, g_ref, wqt_ref, keys_ref, u_ref, vt_ref, o_ref,
                 hb_s, qt_s, sc_s, r1_s, e1_s, cnt_s, a0_s, rows_s, hut_s, coef_s, acc_s, *, tm, ec):
    c = pl.program_id(1)
    nc = pl.num_programs(1)
    nblk = ec // PEER_KEYS
    ncol = tm // LANES

    @pl.when(c == 0)
    def _():
        hb = _rmsnorm(x_ref[...], g_ref[...]).astype(BF)
        hb_s[...] = hb
        qt_s[...] = lax.dot_general(wqt_ref[...], hb, (((1,), (1,)), ((), ())),
                                    preferred_element_type=F32).astype(BF)
        for h in range(PEER_HEADS):
            for p in range(2):
                r0 = (h * 2 + p) * PEER_KEYS
                sc_s[p] = jnp.dot(keys_ref[h * 2 + p], qt_s[r0:r0 + PEER_KEYS, :], preferred_element_type=F32)
            rs = slice(h * PEER_KEYS, (h + 1) * PEER_KEYS)
            for cb in range(ncol):
                ls = slice(cb * LANES, (cb + 1) * LANES)
                s0 = sc_s[0, :, ls]
                s1 = sc_s[1, :, ls]
                def store(rank1, e1, cnt, a0):
                    r1_s[rs, ls] = rank1.astype(r1_s.dtype)
                    e1_s[rs, ls] = e1.astype(e1_s.dtype)
                    cnt_s[rs, ls] = cnt
                    a0_s[rs, ls] = a0

                *fast, ties = _rank_by_value(s0, s1)
                store(*fast)

                @pl.when(jnp.max(ties) > 0.0)
                def _():
                    store(*_rank_stable(s0, s1))
        acc_s[...] = jnp.zeros_like(acc_s)


    for h in range(PEER_HEADS):
        grp = pl.multiple_of(h * PEER_KEYS + c * nblk, SUBLANES)
        rows_s[0, h * nblk:(h + 1) * nblk, :] = cnt_s[pl.ds(grp, nblk), :]
        rows_s[1, h * nblk:(h + 1) * nblk, :] = a0_s[pl.ds(grp, nblk), :]

    sqrt_half = math.sqrt(0.5)
    gdt = r1_s.dtype
    def project(pc):
        es = slice(pc * PEER_PIECE, (pc + 1) * PEER_PIECE)
        hut_s[es, :] = lax.dot_general(u_ref[es, :], hb_s[...], (((1,), (1,)), ((), ())),
                                       preferred_element_type=F32)

    def build(pc):
        for il in range(pc * PEER_PIECE // PEER_KEYS, (pc + 1) * PEER_PIECE // PEER_KEYS):
            bs = slice(il * PEER_KEYS, (il + 1) * PEER_KEYS)
            for cb in range(ncol):
                ls = slice(cb * LANES, (cb + 1) * LANES)
                gate = jnp.zeros((PEER_KEYS, LANES), gdt)
                for h in range(PEER_HEADS):
                    rs = slice(h * PEER_KEYS, (h + 1) * PEER_KEYS)
                    cnt_row = rows_s[0, h * nblk + il:h * nblk + il + 1, ls].astype(gdt)
                    a0_row = rows_s[1, h * nblk + il:h * nblk + il + 1, ls].astype(gdt)
                    keep = jnp.minimum(jnp.maximum(cnt_row - r1_s[rs, ls], 0.0), 1.0)
                    gate = gate + (keep * e1_s[rs, ls]) * a0_row
                hu = hut_s[bs, ls]
                act = 0.5 * hu * (1.0 + lax.erf(hu * sqrt_half))
                coef_s[bs, ls] = (gate * act.astype(gdt)).astype(coef_s.dtype)

    def accumulate(pc):
        es = slice(pc * PEER_PIECE, (pc + 1) * PEER_PIECE)
        acc_s[...] += jnp.dot(vt_ref[:, es], coef_s[es, :], preferred_element_type=F32)

    npc = ec // PEER_PIECE
    for pc in range(min(PEER_LEAD, npc)):
        project(pc)
    for pc in range(npc):
        if pc + PEER_LEAD < npc:
            project(pc + PEER_LEAD)
        build(pc)
        if pc >= PEER_LAG:
            accumulate(pc - PEER_LAG)
    for pc in range(max(npc - PEER_LAG, 0), npc):
        accumulate(pc)

    @pl.when(c == nc - 1)
    def _():
        o_ref[...] = x_ref[...] + acc_s[...].T


def _peer(x, lw, *, tm, ec):
    n = x.shape[0]
    assert n % tm == 0 and tm % MXU_TILE == 0 and ec % (PEER_KEYS * SUBLANES) == 0 and N_EXPERTS % ec == 0
    tok = pl.BlockSpec((tm, D_MODEL), lambda i, c: (i, 0))
    nq = PEER_HEADS * 2 * PEER_KEYS
    nkeys = PEER_HEADS * PEER_KEYS
    return pl.pallas_call(
        functools.partial(_peer_kernel, tm=tm, ec=ec),
        grid=(n // tm, N_EXPERTS // ec),
        in_specs=[tok,
                  pl.BlockSpec((1, D_MODEL), lambda i, c: (0, 0)),
                  pl.BlockSpec((nq, D_MODEL), lambda i, c: (0, 0), pipeline_mode=pl.Buffered(1)),
                  pl.BlockSpec((PEER_HEADS * 2, PEER_KEYS, PEER_KEYS), lambda i, c: (0, 0, 0),
                               pipeline_mode=pl.Buffered(1)),
                  pl.BlockSpec((ec, D_MODEL), lambda i, c: (c, 0)),
                  pl.BlockSpec((D_MODEL, ec), lambda i, c: (0, c))],
        out_specs=tok,
        out_shape=jax.ShapeDtypeStruct((n, D_MODEL), F32),
        scratch_shapes=[
            pltpu.VMEM((tm, D_MODEL), BF),
            pltpu.VMEM((nq, tm), BF),
            pltpu.VMEM((2, PEER_KEYS, tm), F32),
            pltpu.VMEM((nkeys, tm), BF),
            pltpu.VMEM((nkeys, tm), BF),
            pltpu.VMEM((nkeys, tm), F32),
            pltpu.VMEM((nkeys, tm), F32),
            pltpu.VMEM((2, PEER_HEADS * (ec // PEER_KEYS), tm), F32),
            pltpu.VMEM((ec, tm), F32),
            pltpu.VMEM((ec, tm), BF),
            pltpu.VMEM((D_MODEL, tm), F32),
        ],
        compiler_params=pltpu.CompilerParams(
            dimension_semantics=("arbitrary", "arbitrary"), vmem_limit_bytes=VMEM_LIMIT),
        name="peer",
    )(x, lw["norm_ffn"], lw["wq_t"], lw["keys"], lw["u"], lw["v_t"])


def _block_diag_pack(w):
    per = MXU_TILE // RNN_BLOCK
    nsup = w.shape[0] // per
    out = jnp.zeros((nsup, MXU_TILE, MXU_TILE), w.dtype)
    for s in range(nsup):
        for j in range(per):
            out = out.at[s, j * RNN_BLOCK:(j + 1) * RNN_BLOCK, j * RNN_BLOCK:(j + 1) * RNN_BLOCK].set(w[s * per + j])
    return out


def _layer_weights(l, norm_mix, norm_ffn, w_in, conv_w, conv_b, lru_wa, lru_ba, lru_wx, lru_bx, lru_lambda,
                   q_norm, k_norm, attn_sinks, pool_w, pool_scale, gate_b, w_branch, w_out,
                   peer_wq, peer_keys, peer_u, peer_v):
    w = w_in[l]
    o_q = 2 * D_RNN
    o_k = o_q + D_MODEL
    o_v = o_k + D_KV
    o_p = o_v + D_KV
    o_g = o_p + D_POOL
    ones_bd = jnp.asarray(np.kron(np.eye(MXU_TILE // HEAD_DIM), np.ones((HEAD_DIM, HEAD_DIM))), BF)
    return {
        "norm_mix": norm_mix[l][None, :],
        "norm_ffn": norm_ffn[l][None, :],
        "w_seq": jnp.concatenate([w[:, :o_q], w[:, o_p:o_g], w[:, o_k:o_p]], axis=1).astype(BF),
        "w_q": w[:, o_q:o_k].astype(BF),
        "w_g": w[:, o_g:].astype(BF),
        "conv_w": conv_w[l],
        "conv_b": conv_b[l][None, :],
        "wa": _block_diag_pack(lru_wa[l]).astype(BF),
        "wx": _block_diag_pack(lru_wx[l]).astype(BF),
        "ba": lru_ba[l][None, :],
        "bx": lru_bx[l][None, :],
        "lam": lru_lambda[l][None, :],
        "q_norm": jnp.tile(q_norm[l], N_HEADS)[None, :],
        "k_norm": jnp.tile(k_norm[l], N_KV)[None, :],
        "ones_bd": ones_bd,
        "sinks": attn_sinks[l],
        "pool_w": pool_w[l].astype(BF),
        "pool_scale": pool_scale[l][None, :],
        "gate_b": gate_b[l].reshape(1, N_BRANCH * D_MODEL),
        "w_branch": w_branch[l].astype(BF),
        "w_out": w_out[l].astype(BF),
        "wq_t": peer_wq[l].T.astype(BF),
        "keys": peer_keys[l].reshape(PEER_HEADS * 2, PEER_KEYS, PEER_KEYS).astype(BF),
        "u": peer_u[l].astype(BF),
        "v_t": peer_v[l].T.astype(BF),
    }


def kernel(x_prompt, x_sample, state_rglru_h, state_rglru_conv, cache_meta_k, cache_meta_v, cache_win_k, cache_win_v, state_pool, meta_tokens, norm_mix, norm_ffn, w_in, conv_w, conv_b, lru_wa, lru_ba, lru_wx, lru_bx, lru_lambda, q_norm, k_norm, attn_sinks, pool_w, pool_scale, gate_b, w_branch, w_out, peer_wq, peer_keys, peer_u, peer_v):
    bp, seq, _ = x_prompt.shape
    bs, ts, _ = x_sample.shape
    depth = w_in.shape[0]
    dt = x_prompt.dtype
    s_pad = PROMPT_PAD + N_META + seq
    nchunks = s_pad // CHUNK
    cpt = 3 if nchunks % 3 == 0 else 1
    tt_p = cpt * CHUNK

    xp = jnp.concatenate([jnp.zeros((bp, PROMPT_PAD, D_MODEL), dt),
                          jnp.broadcast_to(meta_tokens.astype(dt)[None], (bp, N_META, D_MODEL)), x_prompt], axis=1)
    n_p = bp * s_pad
    n_s = bs * ts
    x = jnp.concatenate([xp.reshape(n_p, D_MODEL), x_sample.reshape(n_s, D_MODEL)], axis=0)
    tm = 2 * n_s
    ec = 2048

    zeros_ch = jnp.zeros((bp, SUBLANES, D_RNN), dt)
    zeros_h = jnp.zeros((bp, 1, D_RNN), dt)
    zeros_ph = jnp.zeros((bp, POOL_MAX, D_POOL), dt)

    outs = {k: [] for k in ("ph", "pc", "pmk", "pmv", "pwk", "pwv", "ppool", "sh", "sc", "sk", "sv", "spool")}
    for l in range(depth):
        lw = _layer_weights(l, norm_mix, norm_ffn, w_in, conv_w, conv_b, lru_wa, lru_ba, lru_wx, lru_bx,
                            lru_lambda, q_norm, k_norm, attn_sinks, pool_w, pool_scale, gate_b, w_branch,
                            w_out, peer_wq, peer_keys, peer_u, peer_v)
        ya, yc, k, v, cs, hl, pst = _seq_mixer(x, lw, zeros_ch, zeros_h, zeros_ph, nb=bp, nt=s_pad // tt_p,
                                               tt=tt_p, row0=0, pad=PROMPT_PAD, pool_hist=0)
        yb = _attn_prompt(x, k, v, lw, nb=bp, cpt=cpt)
        ys_prompt = (ya, yb, yc)
        k4 = k.reshape(bp, s_pad, N_KV, HEAD_DIM)
        v4 = v.reshape(bp, s_pad, N_KV, HEAD_DIM)
        outs["ph"].append(hl[:, 0])
        outs["pc"].append(cs[:, SUBLANES - (CONV_W - 1):])
        outs["pmk"].append(k4[:, PROMPT_PAD:PROMPT_PAD + N_META])
        outs["pmv"].append(v4[:, PROMPT_PAD:PROMPT_PAD + N_META])
        outs["pwk"].append(k4[:, -WINDOW:])
        outs["pwv"].append(v4[:, -WINDOW:])
        outs["ppool"].append(pst[:, 1:])
        ch8 = jnp.pad(state_rglru_conv[l], ((0, 0), (SUBLANES - (CONV_W - 1), 0), (0, 0)))
        ph16 = jnp.pad(state_pool[l], ((0, 0), (1, 0), (0, 0)))
        ya, yc, k, v, cs, hl, pst = _seq_mixer(x, lw, ch8, state_rglru_h[l][:, None, :], ph16, nb=bs, nt=1,
                                               tt=ts, row0=n_p, pad=0, pool_hist=POOL_MAX - 1)
        yb = _attn_sample(x, k, v, cache_meta_k[l].reshape(bs, N_META, D_KV),
                          cache_meta_v[l].reshape(bs, N_META, D_KV),
                          cache_win_k[l].reshape(bs, WINDOW, D_KV), cache_win_v[l].reshape(bs, WINDOW, D_KV),
                          lw, nb=bs, t_new=ts, row0=n_p)
        x = _merge(x, ys_prompt, (ya, yb, yc), lw, tm=tm)
        x = _peer(x, lw, tm=tm, ec=ec)
        outs["sh"].append(hl[:, 0])
        outs["sc"].append(cs[:, SUBLANES - (CONV_W - 1):])
        outs["sk"].append(k.reshape(bs, ts, N_KV, HEAD_DIM))
        outs["sv"].append(v.reshape(bs, ts, N_KV, HEAD_DIM))
        outs["spool"].append(pst[:, 1:])

    y_prompt = x[:n_p].reshape(bp, s_pad, D_MODEL)[:, PROMPT_PAD + N_META:]
    y_sample = x[n_p:].reshape(bs, ts, D_MODEL)
    st = lambda key: jnp.stack(outs[key])
    return (y_prompt, y_sample, st("ph"), st("pc"), st("pmk"), st("pmv"), st("pwk"), st("pwv"), st("ppool"),
            st("sh"), st("sc"), st("sk"), st("sv"), st("spool"))
```
